```python
import math
import jax, jax.numpy as jnp
from jax import lax
import numpy as np

D_MODEL = 1024
BATCH = 2
SEQ = 16384
DEPTH = 1

SSD_HEADS = 16
SSD_HEAD_DIM = 64
SSD_INNER = SSD_HEADS * SSD_HEAD_DIM
SSD_GROUPS = 2
SSD_STATE = 128
SSD_CONV = 4
SSD_CHUNK = 128
CONV_CH = SSD_INNER + 2 * SSD_GROUPS * SSD_STATE
SB_HEADS = 8
SB_HEAD_DIM = 64
SB_INNER = SB_HEADS * SB_HEAD_DIM
SB_BLOCK = 128
N_BRANCH = 2
MEM_LEN = 256
XA_HEADS = 4
XA_HEAD_DIM = 128
XA_INNER = XA_HEADS * XA_HEAD_DIM
N_EXPERTS = 32
TOP_K = 4
D_FF = D_MODEL
SWIGLU_LIMIT = 7.0
SWIGLU_ALPHA = 1.702
MOE_BLOCK = 256
EPS = 1e-6
SPLIT_Z = SSD_INNER
SPLIT_XBC = SPLIT_Z + CONV_CH
SPLIT_DT = SPLIT_XBC + SSD_HEADS
SPLIT_Q = SPLIT_DT + SB_INNER
SPLIT_K = SPLIT_Q + SB_INNER
SPLIT_V = SPLIT_K + SB_INNER
N_IN = SPLIT_V + N_BRANCH * D_MODEL

kernel_name = "hybrid_ssd_stickbreak_moe_block"


def rms_norm(x, g):
    xf = x.astype(jnp.float32)
    y = xf * lax.rsqrt(jnp.mean(xf * xf, axis=-1, keepdims=True) + EPS)
    return (y * g.astype(jnp.float32)).astype(x.dtype)


def causal_dwconv(u, w, b):
    c = u.shape[-1]
    y = lax.conv_general_dilated(u, w[:, None, :].astype(u.dtype), window_strides=(1,),
                                 padding=[(SSD_CONV - 1, 0)],
                                 dimension_numbers=('NWC', 'WIO', 'NWC'),
                                 feature_group_count=c)
    return y + b


def segsum(a):
    t = a.shape[-1]
    cs = jnp.cumsum(a, axis=-1)
    diff = cs[..., :, None] - cs[..., None, :]
    mask = jnp.tril(jnp.ones((t, t), dtype=bool))
    return jnp.where(mask, diff, -jnp.inf)


def ssd_chunked(xh, dt, a, bm, cm):
    bsz, s, h, p = xh.shape
    g, n = bm.shape[-2:]
    r = h // g
    c = s // SSD_CHUNK
    l = SSD_CHUNK
    dtype = xh.dtype
    xdt = (xh * dt[..., None].astype(dtype)).reshape(bsz, c, l, g, r, p)
    da = (dt * a).reshape(bsz, c, l, g, r).transpose(0, 3, 4, 1, 2)
    da_cs = jnp.cumsum(da, axis=-1)
    bc = bm.reshape(bsz, c, l, g, n)
    cc = cm.reshape(bsz, c, l, g, n)
    lmat = jnp.exp(segsum(da)).astype(dtype)
    cb = jnp.einsum('bclgn,bcsgn->bgcls', cc, bc)
    y_diag = jnp.einsum('bgrcls,bcsgrp->bclgrp', cb[:, :, None] * lmat, xdt)
    decay_states = jnp.exp(da_cs[..., -1:] - da_cs).astype(dtype)
    xw = xdt * decay_states.transpose(0, 3, 4, 1, 2)[..., None]
    states = jnp.einsum('bclgn,bclgrp->bcgrpn', bc, xw)
    chunk_tot = jnp.pad(da_cs[..., -1], ((0, 0), (0, 0), (0, 0), (1, 0)))
    decay_chunk = jnp.exp(segsum(chunk_tot)).astype(dtype)
    states0 = jnp.concatenate([jnp.zeros_like(states[:, :1]), states], axis=1)
    new_states = jnp.einsum('bgrzc,bcgrpn->bzgrpn', decay_chunk, states0)
    states_in = new_states[:, :-1]
    out_decay = jnp.exp(da_cs).astype(dtype).transpose(0, 3, 4, 1, 2)[..., None]
    y_off = jnp.einsum('bclgn,bcgrpn->bclgrp', cc, states_in) * out_decay
    return (y_diag + y_off).reshape(bsz, s, h, p)


def ssd_mixer(z, xbc_raw, dt_raw, conv_w, conv_b, dt_bias, a_log, d_skip, g_norm):
    bsz, s, _ = z.shape
    xbc = jax.nn.silu(causal_dwconv(xbc_raw, conv_w, conv_b))
    xs = xbc[..., :SSD_INNER]
    bm = xbc[..., SSD_INNER:SSD_INNER + SSD_GROUPS * SSD_STATE].reshape(bsz, s, SSD_GROUPS, SSD_STATE)
    cm = xbc[..., SSD_INNER + SSD_GROUPS * SSD_STATE:].reshape(bsz, s, SSD_GROUPS, SSD_STATE)
    xh = xs.reshape(bsz, s, SSD_HEADS, SSD_HEAD_DIM)
    dt = jax.nn.softplus(dt_raw.astype(jnp.float32) + dt_bias.astype(jnp.float32))
    a = -jnp.exp(a_log.astype(jnp.float32))
    y = ssd_chunked(xh, dt, a, bm, cm) + xh * d_skip[:, None]
    y = y.reshape(bsz, s, SSD_INNER) * jax.nn.silu(z)
    yg = rms_norm(y.reshape(bsz, s, SSD_GROUPS, SSD_INNER // SSD_GROUPS),
                  g_norm.reshape(SSD_GROUPS, SSD_INNER // SSD_GROUPS))
    return yg.reshape(bsz, s, SSD_INNER)


def stick_breaking_attention(q, k, v):
    bsz, s, h, dh = q.shape
    nb = s // SB_BLOCK
    scale = dh ** -0.5
    kt = k.transpose(0, 2, 1, 3)
    vt = v.transpose(0, 2, 1, 3)
    qb = q.reshape(bsz, nb, SB_BLOCK, h, dh).transpose(1, 0, 3, 2, 4)
    key_pos = jnp.arange(s)

    def block(args):
        q_blk, i = args
        q_pos = i * SB_BLOCK + jnp.arange(SB_BLOCK)
        logits = jnp.einsum('bhtd,bhsd->bhts', q_blk, kt).astype(jnp.float32) * scale
        mask = key_pos[None, :] < q_pos[:, None]
        log_beta = jax.nn.log_sigmoid(logits)
        log_stay = jnp.where(mask, jax.nn.log_sigmoid(-logits), 0.0)
        rev = lax.cumsum(log_stay, axis=3, reverse=True)
        after = jnp.concatenate([rev[..., 1:], jnp.zeros_like(rev[..., :1])], axis=-1)
        w = jnp.where(mask, jnp.exp(log_beta + after), 0.0)
        return jnp.einsum('bhts,bhsd->bthd', w.astype(vt.dtype), vt)

    out = lax.map(block, (qb, jnp.arange(nb)))
    return out.transpose(1, 0, 2, 3, 4).reshape(bsz, s, h * dh)


def memory_cross_attention(hn, memn, w_xq, w_xkv, w_xo):
    bsz, s, _ = hn.shape
    m = memn.shape[1]
    q = (hn @ w_xq).reshape(bsz, s, XA_HEADS, XA_HEAD_DIM)
    kv = memn @ w_xkv
    k = kv[..., :XA_INNER].reshape(bsz, m, XA_HEADS, XA_HEAD_DIM)
    v = kv[..., XA_INNER:].reshape(bsz, m, XA_HEADS, XA_HEAD_DIM)
    logits = jnp.einsum('bthd,bmhd->bhtm', q, k).astype(jnp.float32) * (XA_HEAD_DIM ** -0.5)
    p = jax.nn.softmax(logits, axis=-1).astype(v.dtype)
    o = jnp.einsum('bhtm,bmhd->bthd', p, v).reshape(bsz, s, XA_INNER)
    return o @ w_xo


def moe_ffn(hn, w_router, b_router, w_gu, b_gu, w_down, b_down):
    bsz, s, d = hn.shape
    t = bsz * s
    n_assign = t * TOP_K
    hf = hn.reshape(t, d)
    logits = (hf @ w_router + b_router).astype(jnp.float32)
    top_v, top_i = lax.top_k(logits, TOP_K)
    gate = jax.nn.softmax(top_v, axis=-1)
    e_flat = top_i.reshape(-1)
    tok_flat = jnp.arange(n_assign, dtype=jnp.int32) // TOP_K
    g_flat = gate.reshape(-1)
    order = jnp.argsort(e_flat)
    e_sorted = e_flat[order]
    tok_sorted = tok_flat[order]
    g_sorted = g_flat[order]
    counts = jnp.bincount(e_flat, length=N_EXPERTS)
    padded = (counts + MOE_BLOCK - 1) // MOE_BLOCK * MOE_BLOCK
    start = jnp.cumsum(counts) - counts
    pend = jnp.cumsum(padded)
    pstart = pend - padded
    rank = jnp.arange(n_assign, dtype=jnp.int32) - start[e_sorted]
    dest = pstart[e_sorted] + rank
    n_blocks = -(-n_assign // MOE_BLOCK) + N_EXPERTS
    n_rows = n_blocks * MOE_BLOCK
    row_tok = jnp.full((n_rows,), t, dtype=jnp.int32).at[dest].set(tok_sorted)
    row_gate = jnp.zeros((n_rows,), jnp.float32).at[dest].set(g_sorted)
    blk_start = jnp.arange(n_blocks) * MOE_BLOCK
    blk_expert = jnp.minimum(jnp.searchsorted(pend, blk_start, side='right'), N_EXPERTS - 1)
    xin = jnp.take(hf, row_tok, axis=0, mode='fill', fill_value=0).reshape(n_blocks, MOE_BLOCK, d)

    def expert_block(args):
        xb, e = args
        gu = xb @ w_gu[e] + b_gu[e]
        g_lin = jnp.minimum(gu[..., :D_FF], SWIGLU_LIMIT)
        u_lin = jnp.clip(gu[..., D_FF:], -SWIGLU_LIMIT, SWIGLU_LIMIT)
        act = g_lin * jax.nn.sigmoid(SWIGLU_ALPHA * g_lin) * (u_lin + 1.0)
        return act @ w_down[e] + b_down[e]

    yb = lax.map(expert_block, (xin, blk_expert)).reshape(n_rows, d)
    y = jnp.zeros((t, d), hn.dtype).at[row_tok].add(yb * row_gate[:, None].astype(yb.dtype), mode='drop')
    return y.reshape(bsz, s, d)


def setup_inputs(seed: int = 0) -> dict:
    key = jax.random.key(seed)
    ks = jax.random.split(key, 32)
    f32 = jnp.float32
    nrm = lambda k, shp, sc: jax.random.normal(k, shp, f32) * sc
    gain = lambda k, shp: 1.0 + 0.02 * jax.random.normal(k, shp, f32)
    u = jax.random.uniform(ks[5], (DEPTH, SSD_HEADS), f32)
    dt0 = jnp.exp(u * (math.log(0.1) - math.log(0.001)) + math.log(0.001))
    return {
        "x": nrm(ks[0], (BATCH, SEQ, D_MODEL), 1.0),
        "mem": nrm(ks[1], (BATCH, MEM_LEN, D_MODEL), 1.0),
        "g_mix": gain(ks[2], (DEPTH, D_MODEL)),
        "w_in": nrm(ks[3], (DEPTH, D_MODEL, N_IN), D_MODEL ** -0.5),
        "conv_w": nrm(ks[4], (DEPTH, SSD_CONV, CONV_CH), SSD_CONV ** -0.5),
        "conv_b": nrm(ks[6], (DEPTH, CONV_CH), 0.01),
        "dt_bias": dt0 + jnp.log(-jnp.expm1(-dt0)),
        "a_log": jnp.log(jax.random.uniform(ks[7], (DEPTH, SSD_HEADS), f32, 1.0, 16.0)),
        "d_skip": 1.0 + 0.1 * jax.random.normal(ks[8], (DEPTH, SSD_HEADS), f32),
        "g_ssd": gain(ks[9], (DEPTH, SSD_INNER)),
        "w_ssd_out": nrm(ks[10], (DEPTH, SSD_INNER, D_MODEL), SSD_INNER ** -0.5),
        "w_sb_out": nrm(ks[11], (DEPTH, SB_INNER, D_MODEL), SB_INNER ** -0.5),
        "w_out": nrm(ks[12], (DEPTH, D_MODEL, D_MODEL), D_MODEL ** -0.5),
        "g_xattn": gain(ks[13], (DEPTH, D_MODEL)),
        "g_mem": gain(ks[14], (DEPTH, D_MODEL)),
        "w_xq": nrm(ks[15], (DEPTH, D_MODEL, XA_INNER), D_MODEL ** -0.5),
        "w_xkv": nrm(ks[16], (DEPTH, D_MODEL, 2 * XA_INNER), D_MODEL ** -0.5),
        "w_xo": nrm(ks[17], (DEPTH, XA_INNER, D_MODEL), XA_INNER ** -0.5),
        "g_moe": gain(ks[18], (DEPTH, D_MODEL)),
        "w_router": nrm(ks[19], (DEPTH, D_MODEL, N_EXPERTS), D_MODEL ** -0.5),
        "b_router": nrm(ks[20], (DEPTH, N_EXPERTS), 0.01),
        "w_gu": nrm(ks[21], (DEPTH, N_EXPERTS, D_MODEL, 2 * D_FF), D_MODEL ** -0.5),
        "b_gu": nrm(ks[22], (DEPTH, N_EXPERTS, 2 * D_FF), 0.01),
        "w_down": nrm(ks[23], (DEPTH, N_EXPERTS, D_FF, D_MODEL), D_FF ** -0.5),
        "b_down": nrm(ks[24], (DEPTH, N_EXPERTS, D_MODEL), 0.01),
        "g_final": gain(ks[25], (D_MODEL,)),
    }


def reference(x, mem, g_mix, w_in, conv_w, conv_b, dt_bias, a_log, d_skip, g_ssd, w_ssd_out,
              w_sb_out, w_out, g_xattn, g_mem, w_xq, w_xkv, w_xo, g_moe, w_router, b_router,
              w_gu, b_gu, w_down, b_down, g_final):
    bsz, s, _ = x.shape
    h = x
    for i in range(DEPTH):
        n = rms_norm(h, g_mix[i])
        proj = n @ w_in[i]
        z = proj[..., :SPLIT_Z]
        xbc = proj[..., SPLIT_Z:SPLIT_XBC]
        dt_raw = proj[..., SPLIT_XBC:SPLIT_DT]
        q = proj[..., SPLIT_DT:SPLIT_Q].reshape(bsz, s, SB_HEADS, SB_HEAD_DIM)
        k = proj[..., SPLIT_Q:SPLIT_K].reshape(bsz, s, SB_HEADS, SB_HEAD_DIM)
        v = proj[..., SPLIT_K:SPLIT_V].reshape(bsz, s, SB_HEADS, SB_HEAD_DIM)
        gates = jax.nn.sigmoid(proj[..., SPLIT_V:])
        y_ssd = ssd_mixer(z, xbc, dt_raw, conv_w[i], conv_b[i], dt_bias[i], a_log[i],
                          d_skip[i], g_ssd[i]) @ w_ssd_out[i]
        y_sb = stick_breaking_attention(q, k, v) @ w_sb_out[i]
        merged = gates[..., :D_MODEL] * y_ssd + gates[..., D_MODEL:] * y_sb
        h = h + merged @ w_out[i]
        h = h + memory_cross_attention(rms_norm(h, g_xattn[i]), rms_norm(mem, g_mem[i]),
                                       w_xq[i], w_xkv[i], w_xo[i])
        h = h + moe_ffn(rms_norm(h, g_moe[i]), w_router[i], b_router[i], w_gu[i], b_gu[i],
                        w_down[i], b_down[i])
    return rms_norm(h, g_final)
```

```python
import functools

import jax
import jax.numpy as jnp
from jax import lax
from jax.experimental import pallas as pl
from jax.experimental.pallas import tpu as pltpu

F32 = jnp.float32
BF16 = jnp.bfloat16
I32 = jnp.int32

LANES = 128
SUBLANES = 8
EPS = 1e-6
VMEM_LIMIT = 56 * 1024 * 1024

SSD_HEADS = 16
SSD_HEAD_DIM = 64
SSD_GROUPS = 2
SSD_STATE = 128
SSD_CONV = 4
SB_HEADS = 8
SB_HEAD_DIM = 64
XA_HEADS = 4
XA_HEAD_DIM = 128
TOP_K = 4
SWIGLU_LIMIT = 7.0
SWIGLU_ALPHA = 1.702

SSD_L = 128
SB_TQ = 512
SB_TK = 128
SB_SKIP = 110.0
MOE_BLOCK = 256
NEG_BIG = -1e30


def _dot(a, b):
    return jnp.dot(a, b, preferred_element_type=F32)


def _dot_nt(a, b):
    return lax.dot_general(a, b, (((1,), (1,)), ((), ())), preferred_element_type=F32)


def _split_bf16(v):
    hi = v.astype(BF16)
    lo = (v - hi.astype(F32)).astype(BF16)
    return hi, lo


def _dot2(v, m):
    hi, lo = _split_bf16(v)
    return _dot(hi, m) + _dot(lo, m)


def _softplus(x):
    return jnp.maximum(x, 0.0) + jnp.log(1.0 + jnp.exp(-jnp.abs(x)))


def _sigmoid(x):
    return 1.0 / (1.0 + jnp.exp(-x))


def _params(sem, vmem=VMEM_LIMIT):
    return pltpu.CompilerParams(dimension_semantics=sem, vmem_limit_bytes=vmem)


def _inproj_kernel(x_ref, g_ref, w_ref, wdt_ref, proj_ref, dt_ref, n_scr):
    @pl.when(pl.program_id(1) == 0)
    def _():
        x = x_ref[...]
        n = x * lax.rsqrt(jnp.mean(x * x, axis=-1, keepdims=True) + EPS) * g_ref[...]
        nb = n.astype(BF16)
        n_scr[...] = nb
        dt_ref[...] = _dot(nb, wdt_ref[...])

    proj_ref[...] = _dot(n_scr[...], w_ref[...]).astype(BF16)


def inproj(x2, g, w, wdt, tm=1024, tn=1024):
    t, d = x2.shape
    n = w.shape[1]
    return pl.pallas_call(
        _inproj_kernel,
        grid=(t // tm, n // tn),
        in_specs=[
            pl.BlockSpec((tm, d), lambda i, j: (i, 0)),
            pl.BlockSpec((1, d), lambda i, j: (0, 0)),
            pl.BlockSpec((d, tn), lambda i, j: (0, j)),
            pl.BlockSpec((d, LANES), lambda i, j: (0, 0)),
        ],
        out_specs=[
            pl.BlockSpec((tm, tn), lambda i, j: (i, j)),
            pl.BlockSpec((tm, LANES), lambda i, j: (i, 0)),
        ],
        out_shape=[
            jax.ShapeDtypeStruct((t, n), BF16),
            jax.ShapeDtypeStruct((t, LANES), F32),
        ],
        scratch_shapes=[pltpu.VMEM((tm, d), BF16)],
        compiler_params=_params(("arbitrary", "arbitrary")),
        name="inproj",
    )(x2, g, w, wdt)


def _ssd_kernel(z_ref, xs_ref, bc_ref, dt_ref, cw_ref, cb_ref, dtb_ref, alog_ref, dskip_ref,
                gn_ref, e_ref, o_ref, ext_scr, state_scr):
    L = SSD_L
    inner = SSD_HEADS * SSD_HEAD_DIM
    gw = inner // SSD_GROUPS
    n = SSD_STATE

    @pl.when(pl.program_id(1) == 0)
    def _():
        ext_scr[0:SUBLANES, :] = jnp.zeros((SUBLANES, ext_scr.shape[1]), F32)
        state_scr[...] = jnp.zeros(state_scr.shape, F32)

    ext_scr[SUBLANES:SUBLANES + L, 0:inner] = xs_ref[...].astype(F32)
    ext_scr[SUBLANES:SUBLANES + L, inner:] = bc_ref[...].astype(F32)
    conv = cb_ref[...]
    for k in range(SSD_CONV):
        off = SUBLANES - (SSD_CONV - 1) + k
        conv = conv + cw_ref[k:k + 1, :] * ext_scr[off:off + L, :]
    ext_scr[0:SUBLANES, :] = ext_scr[L:L + SUBLANES, :]
    xbc = conv * _sigmoid(conv)
    xs = xbc[:, :inner]

    dt = _softplus(dt_ref[...] + dtb_ref[...])
    da = dt * (-jnp.exp(alog_ref[...]))
    row = lax.broadcasted_iota(I32, (L, L), 0)
    col = lax.broadcasted_iota(I32, (L, L), 1)
    tril = col <= row
    cs = jnp.dot(tril.astype(F32), da, precision=lax.Precision.HIGHEST,
                 preferred_element_type=F32)
    cst = cs.T
    cs_last = cs[L - 1:L, :]

    e = e_ref[...]
    xdt = xs * _dot2(dt, e)
    xw = (xdt * _dot2(jnp.exp(cs_last - cs), e)).astype(BF16)
    od_e = _dot2(jnp.exp(cs), e)
    xdt_b = xdt.astype(BF16)
    lane = lax.broadcasted_iota(I32, (L, LANES), 1)

    y_parts = []
    new_states = []
    for g in range(SSD_GROUPS):
        bg = xbc[:, inner + g * n:inner + (g + 1) * n]
        cg = xbc[:, inner + SSD_GROUPS * n + g * n:inner + SSD_GROUPS * n + (g + 1) * n]
        bg_b = bg.astype(BF16)
        cg_b = cg.astype(BF16)
        cbm = _dot_nt(cg_b, bg_b)
        st = state_scr[:, g * gw:(g + 1) * gw]
        y_off = _dot(cg_b, st.astype(BF16))
        new_states.append(_dot(bg.T.astype(BF16), xw[:, g * gw:(g + 1) * gw]))
        heads_per_group = SSD_HEADS // SSD_GROUPS
        for hp in range(heads_per_group // 2):
            pair = xdt_b[:, g * gw + hp * LANES:g * gw + (hp + 1) * LANES]
            yd = []
            for sub in range(2):
                h = g * heads_per_group + hp * 2 + sub
                seg = cs[:, h:h + 1] - cst[h:h + 1, :]
                lm = jnp.exp(jnp.where(tril, seg, -jnp.inf))
                yd.append(_dot((cbm * lm).astype(BF16), pair))
            y_parts.append((jnp.where(lane < SSD_HEAD_DIM, yd[0], yd[1]), g, hp, y_off))

    cols = []
    for (yd, g, hp, y_off) in y_parts:
        c0 = g * gw + hp * LANES
        cols.append(yd + y_off[:, hp * LANES:(hp + 1) * LANES] * od_e[:, c0:c0 + LANES])
    y = jnp.concatenate(cols, axis=1) + xs * dskip_ref[...]
    zf = z_ref[...].astype(F32)
    y = y * (zf * _sigmoid(zf))
    outs = []
    for g in range(SSD_GROUPS):
        yg = y[:, g * gw:(g + 1) * gw]
        ms = jnp.mean(yg * yg, axis=-1, keepdims=True)
        outs.append(yg * lax.rsqrt(ms + EPS) * gn_ref[:, g * gw:(g + 1) * gw])
    o_ref[...] = jnp.concatenate(outs, axis=1).astype(BF16)

    cd_e = od_e[L - 1:L, :]
    for g in range(SSD_GROUPS):
        state_scr[:, g * gw:(g + 1) * gw] = (
            state_scr[:, g * gw:(g + 1) * gw] * cd_e[:, g * gw:(g + 1) * gw] + new_states[g])


def ssd(proj, dt_raw, conv_w, conv_b, dtb, alog, dskip_e, gn, e_mat, bsz, seq):
    L = SSD_L
    inner = SSD_HEADS * SSD_HEAD_DIM
    bcw = 2 * SSD_GROUPS * SSD_STATE
    nc = seq // L
    ch = inner + bcw
    full = lambda shape: pl.BlockSpec(shape, lambda b, c: (0,) * len(shape))
    return pl.pallas_call(
        _ssd_kernel,
        grid=(bsz, nc),
        in_specs=[
            pl.BlockSpec((L, inner), lambda b, c: (b * nc + c, 0)),
            pl.BlockSpec((L, inner), lambda b, c: (b * nc + c, 1)),
            pl.BlockSpec((L, bcw), lambda b, c: (b * nc + c, 2 * inner // bcw)),
            pl.BlockSpec((L, LANES), lambda b, c: (b * nc + c, 0)),
            full((SSD_CONV, ch)), full((1, ch)), full((1, LANES)), full((1, LANES)),
            full((1, inner)), full((1, inner)), full((LANES, inner)),
        ],
        out_specs=pl.BlockSpec((L, inner), lambda b, c: (b * nc + c, 0)),
        out_shape=jax.ShapeDtypeStruct((bsz * seq, inner), BF16),
        scratch_shapes=[pltpu.VMEM((L + 2 * SUBLANES, ch), F32), pltpu.VMEM((SSD_STATE, inner), F32)],
        compiler_params=_params(("arbitrary", "arbitrary")),
        name="ssd",
    )(proj, proj, proj, dt_raw, conv_w, conv_b, dtb, alog, dskip_e, gn, e_mat)


def _sb_kernel(q_ref, k_ref, v_ref, u_ref, o_ref, carry_scr, acc_scr):
    tq, tk = SB_TQ, SB_TK
    i = pl.program_id(2)
    lane = lax.broadcasted_iota(I32, (tq, LANES), 1)
    q = q_ref[...]
    qpos = i * tq + lax.broadcasted_iota(I32, (tq, tk), 0)
    kiota = lax.broadcasted_iota(I32, (tq, tk), 1)
    jstart = (i * tq + tq) // tk - 1
    scale = SB_HEAD_DIM ** -0.5
    u = u_ref[...]
    accs = []
    for h in range(LANES // SB_HEAD_DIM):
        in_head = (lane >= h * SB_HEAD_DIM) & (lane < (h + 1) * SB_HEAD_DIM)
        qh = (jnp.where(in_head, q, jnp.zeros_like(q)).astype(F32) * scale).astype(BF16)
        carry_scr[...] = jnp.zeros(carry_scr.shape, F32)
        acc_scr[...] = jnp.zeros(acc_scr.shape, F32)

        def cond(st):
            j, go = st
            return (j >= 0) & (go > 0)

        def body(st):
            j, _ = st
            ks = pl.multiple_of(j * tk, tk)
            kb = k_ref[pl.ds(ks, tk), :]
            vb = v_ref[pl.ds(ks, tk), :]
            x = _dot_nt(qh, kb)
            mask = (j * tk + kiota) < qpos
            sp = jnp.where(mask, _softplus(x), 0.0)
            suf = _dot2(sp, u)
            carry = carry_scr[...]
            w = jnp.where(mask, jnp.exp(x - sp - suf[:, :tk] - carry), 0.0)
            acc_scr[...] += _dot(w.astype(BF16), vb)
            newc = carry + suf[:, tk:]
            carry_scr[...] = newc
            go = (jnp.min(newc) < SB_SKIP).astype(I32)
            return j - 1, go

        lax.while_loop(cond, body, (jstart, jnp.int32(1)))
        accs.append(acc_scr[...])
    o_ref[...] = jnp.where(lane < SB_HEAD_DIM, accs[0], accs[1]).astype(BF16)


def sb_attn(proj, u_mat, bsz, seq, qcol, kcol, vcol):
    tq, tk = SB_TQ, SB_TK
    nq = seq // tq
    npairs = SB_HEADS * SB_HEAD_DIM // LANES
    return pl.pallas_call(
        _sb_kernel,
        grid=(bsz, npairs, nq),
        in_specs=[
            pl.BlockSpec((tq, LANES), lambda b, p, i: (b * nq + i, qcol + p)),
            pl.BlockSpec((seq, LANES), lambda b, p, i: (b, kcol + p)),
            pl.BlockSpec((seq, LANES), lambda b, p, i: (b, vcol + p)),
            pl.BlockSpec((tk, tk + LANES), lambda b, p, i: (0, 0)),
        ],
        out_specs=pl.BlockSpec((tq, LANES), lambda b, p, i: (b * nq + i, p)),
        out_shape=jax.ShapeDtypeStruct((bsz * seq, npairs * LANES), BF16),
        scratch_shapes=[pltpu.VMEM((tq, LANES), F32), pltpu.VMEM((tq, LANES), F32)],
        compiler_params=_params(("arbitrary", "arbitrary", "arbitrary")),
        name="sb_attn",
    )(proj, proj, proj, u_mat)


def _memkv_kernel(m_ref, g_ref, w_ref, o_ref):
    x = m_ref[...]
    n = x * lax.rsqrt(jnp.mean(x * x, axis=-1, keepdims=True) + EPS) * g_ref[...]
    o_ref[...] = _dot(n.astype(BF16), w_ref[...]).astype(BF16)


def mem_kv(mem2, g, w, mlen):
    rows, d = mem2.shape
    n = w.shape[1]
    return pl.pallas_call(
        _memkv_kernel,
        grid=(rows // mlen,),
        in_specs=[pl.BlockSpec((mlen, d), lambda b: (b, 0)),
                  pl.BlockSpec((1, d), lambda b: (0, 0)),
                  pl.BlockSpec((d, n), lambda b: (0, 0))],
        out_specs=pl.BlockSpec((mlen, n), lambda b: (b, 0)),
        out_shape=jax.ShapeDtypeStruct((rows, n), BF16),
        compiler_params=_params(("arbitrary",)),
        name="mem_kv",
    )(mem2, g, w)


def _merge_kernel(x_ref, yssd_ref, ysb_ref, gssd_ref, gsb_ref, wso_ref, wsbo_ref, wo_ref, gx_ref,
                  wxq_ref, kv_ref, wxo_ref, gm_ref, wrh_ref, wrl_ref, br_ref, tri_ref,
                  h2_ref, hn_ref, eidx_ref, rank_ref, gate_ref, cnt_ref, cnt_scr):
    @pl.when(pl.program_id(0) == 0)
    def _():
        cnt_scr[...] = jnp.zeros(cnt_scr.shape, F32)

    tm = x_ref.shape[0]
    y_ssd = _dot(yssd_ref[...], wso_ref[...])
    y_sb = _dot(ysb_ref[...], wsbo_ref[...])
    merged = (_sigmoid(gssd_ref[...].astype(F32)) * y_ssd
              + _sigmoid(gsb_ref[...].astype(F32)) * y_sb)
    h1 = x_ref[...] + _dot(merged.astype(BF16), wo_ref[...])

    n2 = h1 * lax.rsqrt(jnp.mean(h1 * h1, axis=-1, keepdims=True) + EPS) * gx_ref[...]
    q = (_dot(n2.astype(BF16), wxq_ref[...]) * (XA_HEAD_DIM ** -0.5)).astype(BF16)
    xi = XA_HEADS * XA_HEAD_DIM
    heads = []
    for h in range(XA_HEADS):
        kh = kv_ref[:, h * XA_HEAD_DIM:(h + 1) * XA_HEAD_DIM]
        vh = kv_ref[:, xi + h * XA_HEAD_DIM:xi + (h + 1) * XA_HEAD_DIM]
        lg = _dot_nt(q[:, h * XA_HEAD_DIM:(h + 1) * XA_HEAD_DIM], kh)
        ex = jnp.exp(lg - jnp.max(lg, axis=-1, keepdims=True))
        den = jnp.sum(ex, axis=-1, keepdims=True)
        heads.append(_dot(ex.astype(BF16), vh) / den)
    o = jnp.concatenate(heads, axis=1).astype(BF16)
    h2 = h1 + _dot(o, wxo_ref[...])
    h2_ref[...] = h2

    n3 = h2 * lax.rsqrt(jnp.mean(h2 * h2, axis=-1, keepdims=True) + EPS) * gm_ref[...]
    for s in range(n3.shape[1] // LANES):
        hn_ref[:, s, :] = n3[:, s * LANES:(s + 1) * LANES]

    nh, nl = _split_bf16(n3)
    lg = _dot(nh, wrh_ref[...]) + _dot(nl, wrh_ref[...]) + _dot(nh, wrl_ref[...]) + br_ref[...]
    lane = lax.broadcasted_iota(I32, (tm, LANES), 1)
    lanef = lane.astype(F32)
    work = lg
    vals, idxs = [], []
    for _ in range(TOP_K):
        m = jnp.max(work, axis=-1, keepdims=True)
        idx = jnp.min(jnp.where(work == m, lanef, float(LANES)), axis=-1, keepdims=True)
        vals.append(m)
        idxs.append(idx)
        work = jnp.where(lanef == idx, NEG_BIG * 2.0, work)
    exps = [jnp.exp(v - vals[0]) for v in vals]
    den = exps[0]
    for ex in exps[1:]:
        den = den + ex

    sel = jnp.zeros((tm, LANES), F32)
    for idx in idxs:
        sel = sel + (lanef == idx).astype(F32)
    incl = _dot(tri_ref[...], sel.astype(BF16))
    before = incl - sel + cnt_scr[...]
    eidx = jnp.zeros((tm, LANES), F32)
    rank = jnp.zeros((tm, LANES), F32)
    gate = jnp.zeros((tm, LANES), F32)
    for k in range(TOP_K):
        rk = jnp.sum(jnp.where(lanef == idxs[k], before, 0.0), axis=-1, keepdims=True)
        eidx = jnp.where(lane == k, idxs[k], eidx)
        rank = jnp.where(lane == k, rk, rank)
        gate = jnp.where(lane == k, exps[k] / den, gate)
    eidx_ref[...] = eidx.astype(I32)
    rank_ref[...] = rank.astype(I32)
    gate_ref[...] = gate
    cnt_scr[...] = cnt_scr[...] + incl[tm - 1:tm, :]
    cnt_ref[...] = cnt_scr[...].astype(I32)


def merge(x2, y_ssd, y_sb, proj, gcol, wso, wsbo, wo, gx, wxq, kv, wxo, gm, wrh, wrl, br, tri,
          seq, mlen, tm):
    t, d = x2.shape
    si = y_ssd.shape[1]
    sbi = y_sb.shape[1]
    nsub = d // LANES
    xi2 = kv.shape[1]
    full = lambda a: pl.BlockSpec(a.shape, lambda i: (0,) * a.ndim)
    row = lambda w: pl.BlockSpec((tm, w), lambda i: (i, 0))
    return pl.pallas_call(
        _merge_kernel,
        grid=(t // tm,),
        in_specs=[
            row(d), row(si), row(sbi),
            pl.BlockSpec((tm, d), lambda i: (i, gcol)),
            pl.BlockSpec((tm, d), lambda i: (i, gcol + 1)),
            full(wso), full(wsbo), full(wo), full(gx), full(wxq),
            pl.BlockSpec((mlen, xi2), lambda i: ((i * tm) // seq, 0)),
            full(wxo), full(gm), full(wrh), full(wrl), full(br), full(tri),
        ],
        out_specs=[
            row(d),
            pl.BlockSpec((tm, nsub, LANES), lambda i: (i, 0, 0)),
            row(LANES), row(LANES), row(LANES),
            pl.BlockSpec((1, LANES), lambda i: (0, 0)),
        ],
        out_shape=[
            jax.ShapeDtypeStruct((t, d), F32),
            jax.ShapeDtypeStruct((t, nsub, LANES), F32),
            jax.ShapeDtypeStruct((t, LANES), I32),
            jax.ShapeDtypeStruct((t, LANES), I32),
            jax.ShapeDtypeStruct((t, LANES), F32),
            jax.ShapeDtypeStruct((1, LANES), I32),
        ],
        scratch_shapes=[pltpu.VMEM((1, LANES), F32)],
        compiler_params=_params(("arbitrary",)),
        name="merge",
    )(x2, y_ssd, y_sb, proj, proj, wso, wsbo, wo, gx, wxq, kv, wxo, gm, wrh, wrl, br, tri)


def _dispatch_kernel(ps_ref, e_hbm, r_hbm, hn_ref, xin_in, xin_out, e_sm, r_sm, isem, sem):
    del xin_in
    tm = hn_ref.shape[0]
    n = tm * TOP_K
    i = pl.program_id(0)
    ce = pltpu.make_async_copy(e_hbm.at[pl.ds(i * n, n)], e_sm, isem.at[0])
    cr = pltpu.make_async_copy(r_hbm.at[pl.ds(i * n, n)], r_sm, isem.at[1])
    ce.start()
    cr.start()
    ce.wait()
    cr.wait()

    def issue(tok, c):
        for k in range(TOP_K):
            a = tok * TOP_K + k
            dst = ps_ref[e_sm[a]] + r_sm[a]
            pltpu.make_async_copy(hn_ref.at[tok], xin_out.at[dst], sem).start()
        return c

    lax.fori_loop(0, tm, issue, 0)

    def drain(tok, c):
        for k in range(TOP_K):
            pltpu.make_async_copy(hn_ref.at[0], xin_out.at[0], sem).wait()
        return c

    lax.fori_loop(0, tm, drain, 0)


def dispatch(pstart, e_flat, r_flat, hn, xin0, tm=256):
    t = hn.shape[0]
    grid_spec = pltpu.PrefetchScalarGridSpec(
        num_scalar_prefetch=1,
        grid=(t // tm,),
        in_specs=[
            pl.BlockSpec(memory_space=pl.ANY),
            pl.BlockSpec(memory_space=pl.ANY),
            pl.BlockSpec((tm,) + hn.shape[1:], lambda i, ps: (i, 0, 0)),
            pl.BlockSpec(memory_space=pl.ANY),
        ],
        out_specs=pl.BlockSpec(memory_space=pl.ANY),
        scratch_shapes=[pltpu.SMEM((tm * TOP_K,), I32), pltpu.SMEM((tm * TOP_K,), I32),
                        pltpu.SemaphoreType.DMA((2,)), pltpu.SemaphoreType.DMA],
    )
    return pl.pallas_call(
        _dispatch_kernel,
        grid_spec=grid_spec,
        out_shape=jax.ShapeDtypeStruct(xin0.shape, xin0.dtype),
        input_output_aliases={4: 0},
        compiler_params=_params(("arbitrary",)),
        name="dispatch",
    )(pstart, e_flat, r_flat, hn, xin0)


def _moe_kernel(be_ref, nu_ref, x_ref, wgu_ref, bgu_ref, wd_ref, bd_ref, o_ref):
    del be_ref
    i = pl.program_id(0)
    nsub = x_ref.shape[1]
    dff = wd_ref.shape[1]

    @pl.when(i < nu_ref[0])
    def _():
        x = jnp.concatenate([x_ref[:, s, :] for s in range(nsub)], axis=1).astype(BF16)
        gu = _dot(x, wgu_ref[0]) + bgu_ref[0]
        g_lin = jnp.minimum(gu[:, :dff], SWIGLU_LIMIT)
        u_lin = jnp.clip(gu[:, dff:], -SWIGLU_LIMIT, SWIGLU_LIMIT)
        act = g_lin * _sigmoid(SWIGLU_ALPHA * g_lin) * (u_lin + 1.0)
        y = _dot(act.astype(BF16), wd_ref[0]) + bd_ref[0]
        for s in range(nsub):
            o_ref[:, s, :] = y[:, s * LANES:(s + 1) * LANES]

    @pl.when(i >= nu_ref[0])
    def _():
        o_ref[...] = jnp.zeros(o_ref.shape, F32)


def moe_ffn(blk_expert, n_used, xin, wgu, bgu, wd, bd):
    n_rows, nsub, _ = xin.shape
    nb = n_rows // MOE_BLOCK
    d = wgu.shape[1]
    f2 = wgu.shape[2]
    dff = wd.shape[1]
    grid_spec = pltpu.PrefetchScalarGridSpec(
        num_scalar_prefetch=2,
        grid=(nb,),
        in_specs=[
            pl.BlockSpec((MOE_BLOCK, nsub, LANES), lambda i, be, nu: (jnp.minimum(i, nu[0] - 1), 0, 0)),
            pl.BlockSpec((1, d, f2), lambda i, be, nu: (be[i], 0, 0)),
            pl.BlockSpec((1, 1, f2), lambda i, be, nu: (be[i], 0, 0)),
            pl.BlockSpec((1, dff, d), lambda i, be, nu: (be[i], 0, 0)),
            pl.BlockSpec((1, 1, d), lambda i, be, nu: (be[i], 0, 0)),
        ],
        out_specs=pl.BlockSpec((MOE_BLOCK, nsub, LANES), lambda i, be, nu: (i, 0, 0)),
    )
    return pl.pallas_call(
        _moe_kernel,
        grid_spec=grid_spec,
        out_shape=jax.ShapeDtypeStruct(xin.shape, F32),
        compiler_params=_params(("arbitrary",)),
        name="moe_ffn",
    )(blk_expert, n_used, xin, wgu, bgu, wd, bd)


def _combine_kernel(ps_ref, e_hbm, r_hbm, yb_hbm, h2_ref, gate_ref, gf_ref, o_ref,
                    e_sm, r_sm, buf, isem, sem):
    tm = h2_ref.shape[0]
    nsub = buf.shape[2]
    n = tm * TOP_K
    i = pl.program_id(0)
    ce = pltpu.make_async_copy(e_hbm.at[pl.ds(i * n, n)], e_sm, isem.at[0])
    cr = pltpu.make_async_copy(r_hbm.at[pl.ds(i * n, n)], r_sm, isem.at[1])
    ce.start()
    cr.start()
    ce.wait()
    cr.wait()

    def issue(tok, c):
        for k in range(TOP_K):
            a = tok * TOP_K + k
            src = ps_ref[e_sm[a]] + r_sm[a]
            pltpu.make_async_copy(yb_hbm.at[src], buf.at[k, tok], sem).start()
        return c

    lax.fori_loop(0, tm, issue, 0)

    def drain(tok, c):
        for k in range(TOP_K):
            pltpu.make_async_copy(yb_hbm.at[0], buf.at[0, 0], sem).wait()
        return c

    lax.fori_loop(0, tm, drain, 0)

    gate = gate_ref[...]
    cols = []
    for s in range(nsub):
        acc = jnp.zeros((tm, LANES), F32)
        for k in range(TOP_K):
            acc = acc + gate[:, k:k + 1] * buf[k, :, s, :]
        cols.append(acc)
    h3 = h2_ref[...] + jnp.concatenate(cols, axis=1)
    o_ref[...] = h3 * lax.rsqrt(jnp.mean(h3 * h3, axis=-1, keepdims=True) + EPS) * gf_ref[...]


def combine(pstart, e_flat, r_flat, yb, h2, gate, gf, tm=256):
    t, d = h2.shape
    nsub = yb.shape[1]
    grid_spec = pltpu.PrefetchScalarGridSpec(
        num_scalar_prefetch=1,
        grid=(t // tm,),
        in_specs=[
            pl.BlockSpec(memory_space=pl.ANY),
            pl.BlockSpec(memory_space=pl.ANY),
            pl.BlockSpec(memory_space=pl.ANY),
            pl.BlockSpec((tm, d), lambda i, ps: (i, 0)),
            pl.BlockSpec((tm, LANES), lambda i, ps: (i, 0)),
            pl.BlockSpec((1, d), lambda i, ps: (0, 0)),
        ],
        out_specs=pl.BlockSpec((tm, d), lambda i, ps: (i, 0)),
        scratch_shapes=[pltpu.SMEM((tm * TOP_K,), I32), pltpu.SMEM((tm * TOP_K,), I32),
                        pltpu.VMEM((TOP_K, tm, nsub, LANES), F32),
                        pltpu.SemaphoreType.DMA((2,)), pltpu.SemaphoreType.DMA],
    )
    return pl.pallas_call(
        _combine_kernel,
        grid_spec=grid_spec,
        out_shape=jax.ShapeDtypeStruct((t, d), F32),
        compiler_params=_params(("arbitrary",)),
        name="combine",
    )(pstart, e_flat, r_flat, yb, h2, gate, gf)


def _pad_lanes(v, fill=0.0):
    return jnp.pad(v.reshape(1, -1).astype(F32), ((0, 0), (0, LANES - v.shape[-1])), constant_values=fill)


def _layer(h, mem, p, merge_tm=512):
    bsz, seq, d = h.shape
    t = bsz * seq
    mlen = mem.shape[1]
    inner = SSD_HEADS * SSD_HEAD_DIM
    conv_ch = inner + 2 * SSD_GROUPS * SSD_STATE
    sbi = SB_HEADS * SB_HEAD_DIM
    n_experts = p["w_router"].shape[1]
    x2 = h.reshape(t, d)

    w_in = p["w_in"]
    dt0 = inner + conv_ch
    w_main = jnp.concatenate([w_in[:, :dt0], w_in[:, dt0 + SSD_HEADS:]], axis=1).astype(BF16)
    w_dt = jnp.pad(w_in[:, dt0:dt0 + SSD_HEADS], ((0, 0), (0, LANES - SSD_HEADS))).astype(BF16)
    proj, dt_raw = inproj(x2, p["g_mix"].reshape(1, d), w_main, w_dt)
    qcol = dt0 // LANES
    kcol = qcol + sbi // LANES
    vcol = kcol + sbi // LANES
    gcol = (dt0 + 3 * sbi) // d

    heads = jnp.arange(LANES)[:, None]
    chan = jnp.arange(inner)[None, :]
    e_mat = (chan // SSD_HEAD_DIM == heads).astype(BF16)
    y_ssd = ssd(proj, dt_raw, p["conv_w"], p["conv_b"].reshape(1, conv_ch), _pad_lanes(p["dt_bias"]),
                _pad_lanes(p["a_log"]), jnp.repeat(p["d_skip"], SSD_HEAD_DIM).reshape(1, inner),
                p["g_ssd"].reshape(1, inner), e_mat, bsz, seq)

    jj = jnp.arange(SB_TK)[:, None]
    ss = jnp.arange(SB_TK + LANES)[None, :]
    u_mat = ((ss >= SB_TK) | (jj > ss)).astype(BF16)
    y_sb = sb_attn(proj, u_mat, bsz, seq, qcol, kcol, vcol)

    kv = mem_kv(mem.reshape(bsz * mlen, d), p["g_mem"].reshape(1, d), p["w_xkv"].astype(BF16), mlen)

    wr = jnp.pad(p["w_router"], ((0, 0), (0, LANES - n_experts)))
    wrh = wr.astype(BF16)
    wrl = (wr - wrh.astype(F32)).astype(BF16)
    tri = (jnp.arange(merge_tm)[None, :] <= jnp.arange(merge_tm)[:, None]).astype(BF16)
    h2, hn, eidx, rank, gate, counts = merge(
        x2, y_ssd, y_sb, proj, gcol, p["w_ssd_out"].astype(BF16), p["w_sb_out"].astype(BF16),
        p["w_out"].astype(BF16), p["g_xattn"].reshape(1, d), p["w_xq"].astype(BF16), kv,
        p["w_xo"].astype(BF16), p["g_moe"].reshape(1, d), wrh, wrl,
        _pad_lanes(p["b_router"], NEG_BIG), tri, seq, mlen, merge_tm)

    cnt = counts[0, :n_experts]
    padded = (cnt + MOE_BLOCK - 1) // MOE_BLOCK * MOE_BLOCK
    pend = jnp.cumsum(padded)
    pstart = (pend - padded).astype(I32)
    n_blocks = -(-(t * TOP_K) // MOE_BLOCK) + n_experts
    n_used = (pend[-1] // MOE_BLOCK).astype(I32)
    blk = jnp.minimum(jnp.arange(n_blocks, dtype=I32), n_used - 1) * MOE_BLOCK
    blk_expert = jnp.minimum(jnp.searchsorted(pend, blk, side="right"), n_experts - 1).astype(I32)
    e_flat = eidx[:, :TOP_K].reshape(-1)
    r_flat = rank[:, :TOP_K].reshape(-1)

    xin = dispatch(pstart, e_flat, r_flat, hn, jnp.zeros((n_blocks * MOE_BLOCK,) + hn.shape[1:], F32))
    yb = moe_ffn(blk_expert, n_used.reshape(1), xin, p["w_gu"].astype(BF16),
                 p["b_gu"][:, None, :], p["w_down"].astype(BF16), p["b_down"][:, None, :])
    return h2, (pstart, e_flat, r_flat, yb, gate)


def kernel(x, mem, g_mix, w_in, conv_w, conv_b, dt_bias, a_log, d_skip, g_ssd, w_ssd_out, w_sb_out,
           w_out, g_xattn, g_mem, w_xq, w_xkv, w_xo, g_moe, w_router, b_router, w_gu, b_gu, w_down,
           b_down, g_final):
    stacked = dict(g_mix=g_mix, w_in=w_in, conv_w=conv_w, conv_b=conv_b, dt_bias=dt_bias, a_log=a_log,
                   d_skip=d_skip, g_ssd=g_ssd, w_ssd_out=w_ssd_out, w_sb_out=w_sb_out, w_out=w_out,
                   g_xattn=g_xattn, g_mem=g_mem, w_xq=w_xq, w_xkv=w_xkv, w_xo=w_xo, g_moe=g_moe,
                   w_router=w_router, b_router=b_router, w_gu=w_gu, b_gu=b_gu, w_down=w_down,
                   b_down=b_down)
    assert g_mix.shape[0] == 1, "stacked depth other than 1 is not supported"
    bsz, seq, d = x.shape
    p = {k: v[0] for k, v in stacked.items()}
    h2, (pstart, e_flat, r_flat, yb, gate) = _layer(x, mem, p)
    out = combine(pstart, e_flat, r_flat, yb, h2, gate, g_final.reshape(1, d))
    return out.reshape(bsz, seq, d)
```

```python
import functools

import jax
import jax.numpy as jnp
from jax import lax
from jax.experimental import pallas as pl
from jax.experimental.pallas import tpu as pltpu

F32 = jnp.float32
BF16 = jnp.bfloat16
I32 = jnp.int32

LANES = 128
SUBLANES = 8
EPS = 1e-6
VMEM_LIMIT = 56 * 1024 * 1024

SSD_HEADS = 16
SSD_HEAD_DIM = 64
SSD_GROUPS = 2
SSD_STATE = 128
SSD_CONV = 4
SB_HEADS = 8
SB_HEAD_DIM = 64
XA_HEADS = 4
XA_HEAD_DIM = 128
TOP_K = 4
SWIGLU_LIMIT = 7.0
SWIGLU_ALPHA = 1.702

SSD_L = 128
SB_T = 256
SB_SKIP = 110.0
MOE_BLOCK = 512
MOE_CAST_ROWS = 128
NEG_BIG = -1e30


def _dot(a, b):
    return jnp.dot(a, b, preferred_element_type=F32)


def _dot_nt(a, b):
    return lax.dot_general(a, b, (((1,), (1,)), ((), ())), preferred_element_type=F32)


def _split_bf16(v):
    hi = v.astype(BF16)
    lo = (v - hi.astype(F32)).astype(BF16)
    return hi, lo


def _dot2(v, m):
    hi, lo = _split_bf16(v)
    return _dot(hi, m) + _dot(lo, m)


def _softplus(x):
    return jnp.maximum(x, 0.0) + jnp.log(1.0 + jnp.exp(-jnp.abs(x)))


def _sigmoid(x):
    return 1.0 / (1.0 + jnp.exp(-x))


def _params(sem, vmem=VMEM_LIMIT):
    return pltpu.CompilerParams(dimension_semantics=sem, vmem_limit_bytes=vmem)


def _inproj_kernel(x_ref, g_ref, w_ref, wdt_ref, proj_ref, dt_ref, n_scr):
    @pl.when(pl.program_id(1) == 0)
    def _():
        x = x_ref[...]
        n = x * lax.rsqrt(jnp.mean(x * x, axis=-1, keepdims=True) + EPS) * g_ref[...]
        nb = n.astype(BF16)
        n_scr[...] = nb
        dt_ref[...] = _dot(nb, wdt_ref[...])

    proj_ref[...] = _dot(n_scr[...], w_ref[...]).astype(BF16)


def inproj(x2, g, w, wdt, tm=1024, tn=1024):
    t, d = x2.shape
    n = w.shape[1]
    return pl.pallas_call(
        _inproj_kernel,
        grid=(t // tm, n // tn),
        in_specs=[
            pl.BlockSpec((tm, d), lambda i, j: (i, 0)),
            pl.BlockSpec((1, d), lambda i, j: (0, 0)),
            pl.BlockSpec((d, tn), lambda i, j: (0, j)),
            pl.BlockSpec((d, LANES), lambda i, j: (0, 0)),
        ],
        out_specs=[
            pl.BlockSpec((tm, tn), lambda i, j: (i, j)),
            pl.BlockSpec((tm, LANES), lambda i, j: (i, 0)),
        ],
        out_shape=[
            jax.ShapeDtypeStruct((t, n), BF16),
            jax.ShapeDtypeStruct((t, LANES), F32),
        ],
        scratch_shapes=[pltpu.VMEM((tm, d), BF16)],
        compiler_params=_params(("arbitrary", "arbitrary")),
        name="inproj",
    )(x2, g, w, wdt)


def _ssd_kernel(z_ref, xs_ref, bc_ref, dt_ref, cw_ref, cb_ref, dtb_ref, alog_ref, dskip_ref,
                gn_ref, e_ref, o_ref, ext_scr, state_scr):
    L = SSD_L
    inner = SSD_HEADS * SSD_HEAD_DIM
    gw = inner // SSD_GROUPS
    n = SSD_STATE

    @pl.when(pl.program_id(1) == 0)
    def _():
        ext_scr[0:SUBLANES, :] = jnp.zeros((SUBLANES, ext_scr.shape[1]), F32)
        state_scr[...] = jnp.zeros(state_scr.shape, F32)

    ext_scr[SUBLANES:SUBLANES + L, 0:inner] = xs_ref[...].astype(F32)
    ext_scr[SUBLANES:SUBLANES + L, inner:] = bc_ref[...].astype(F32)
    conv = cb_ref[...]
    for k in range(SSD_CONV):
        off = SUBLANES - (SSD_CONV - 1) + k
        conv = conv + cw_ref[k:k + 1, :] * ext_scr[off:off + L, :]
    ext_scr[0:SUBLANES, :] = ext_scr[L:L + SUBLANES, :]
    xbc = conv * _sigmoid(conv)
    xs = xbc[:, :inner]

    dt = _softplus(dt_ref[...] + dtb_ref[...])
    da = dt * (-jnp.exp(alog_ref[...]))
    row = lax.broadcasted_iota(I32, (L, L), 0)
    col = lax.broadcasted_iota(I32, (L, L), 1)
    tril = col <= row
    cs = jnp.dot(tril.astype(F32), da, precision=lax.Precision.HIGHEST,
                 preferred_element_type=F32)
    cst = cs.T
    cs_last = cs[L - 1:L, :]

    e = e_ref[...]
    xdt = xs * _dot2(dt, e)
    xw = (xdt * _dot2(jnp.exp(cs_last - cs), e)).astype(BF16)
    od_e = _dot2(jnp.exp(cs), e)
    xdt_b = xdt.astype(BF16)
    lane = lax.broadcasted_iota(I32, (L, LANES), 1)

    y_parts = []
    new_states = []
    for g in range(SSD_GROUPS):
        bg = xbc[:, inner + g * n:inner + (g + 1) * n]
        cg = xbc[:, inner + SSD_GROUPS * n + g * n:inner + SSD_GROUPS * n + (g + 1) * n]
        bg_b = bg.astype(BF16)
        cg_b = cg.astype(BF16)
        cbm = _dot_nt(cg_b, bg_b)
        st = state_scr[:, g * gw:(g + 1) * gw]
        y_off = _dot(cg_b, st.astype(BF16))
        new_states.append(_dot(bg.T.astype(BF16), xw[:, g * gw:(g + 1) * gw]))
        heads_per_group = SSD_HEADS // SSD_GROUPS
        for hp in range(heads_per_group // 2):
            pair = xdt_b[:, g * gw + hp * LANES:g * gw + (hp + 1) * LANES]
            yd = []
            for sub in range(2):
                h = g * heads_per_group + hp * 2 + sub
                seg = cs[:, h:h + 1] - cst[h:h + 1, :]
                lm = jnp.exp(jnp.where(tril, seg, -jnp.inf))
                yd.append(_dot((cbm * lm).astype(BF16), pair))
            y_parts.append((jnp.where(lane < SSD_HEAD_DIM, yd[0], yd[1]), g, hp, y_off))

    cols = []
    for (yd, g, hp, y_off) in y_parts:
        c0 = g * gw + hp * LANES
        cols.append(yd + y_off[:, hp * LANES:(hp + 1) * LANES] * od_e[:, c0:c0 + LANES])
    y = jnp.concatenate(cols, axis=1) + xs * dskip_ref[...]
    zf = z_ref[...].astype(F32)
    y = y * (zf * _sigmoid(zf))
    outs = []
    for g in range(SSD_GROUPS):
        yg = y[:, g * gw:(g + 1) * gw]
        ms = jnp.mean(yg * yg, axis=-1, keepdims=True)
        outs.append(yg * lax.rsqrt(ms + EPS) * gn_ref[:, g * gw:(g + 1) * gw])
    o_ref[...] = jnp.concatenate(outs, axis=1).astype(BF16)

    cd_e = od_e[L - 1:L, :]
    for g in range(SSD_GROUPS):
        state_scr[:, g * gw:(g + 1) * gw] = (
            state_scr[:, g * gw:(g + 1) * gw] * cd_e[:, g * gw:(g + 1) * gw] + new_states[g])


def ssd(proj, dt_raw, conv_w, conv_b, dtb, alog, dskip_e, gn, e_mat, bsz, seq):
    L = SSD_L
    inner = SSD_HEADS * SSD_HEAD_DIM
    bcw = 2 * SSD_GROUPS * SSD_STATE
    nc = seq // L
    ch = inner + bcw
    full = lambda shape: pl.BlockSpec(shape, lambda b, c: (0,) * len(shape))
    return pl.pallas_call(
        _ssd_kernel,
        grid=(bsz, nc),
        in_specs=[
            pl.BlockSpec((L, inner), lambda b, c: (b * nc + c, 0)),
            pl.BlockSpec((L, inner), lambda b, c: (b * nc + c, 1)),
            pl.BlockSpec((L, bcw), lambda b, c: (b * nc + c, 2 * inner // bcw)),
            pl.BlockSpec((L, LANES), lambda b, c: (b * nc + c, 0)),
            full((SSD_CONV, ch)), full((1, ch)), full((1, LANES)), full((1, LANES)),
            full((1, inner)), full((1, inner)), full((LANES, inner)),
        ],
        out_specs=pl.BlockSpec((L, inner), lambda b, c: (b * nc + c, 0)),
        out_shape=jax.ShapeDtypeStruct((bsz * seq, inner), BF16),
        scratch_shapes=[pltpu.VMEM((L + 2 * SUBLANES, ch), F32), pltpu.VMEM((SSD_STATE, inner), F32)],
        compiler_params=_params(("arbitrary", "arbitrary")),
        name="ssd",
    )(proj, proj, proj, dt_raw, conv_w, conv_b, dtb, alog, dskip_e, gn, e_mat)


def _sb_kernel(q_ref, k_ref, v_ref, u_ref, o_ref, carry_scr, acc_scr):
    t = SB_T
    nh = LANES // SB_HEAD_DIM
    i = pl.program_id(2)
    lane = lax.broadcasted_iota(I32, (t, LANES), 1)
    q = q_ref[...]
    scale = SB_HEAD_DIM ** -0.5
    qh = []
    for h in range(nh):
        in_head = (lane >= h * SB_HEAD_DIM) & (lane < (h + 1) * SB_HEAD_DIM)
        qh.append((jnp.where(in_head, q, jnp.zeros_like(q)).astype(F32) * scale).astype(BF16))
    u = u_ref[...]
    carry_scr[...] = jnp.zeros(carry_scr.shape, F32)
    acc_scr[...] = jnp.zeros(acc_scr.shape, F32)

    def step(j, diagonal):
        ks = pl.multiple_of(j * t, t)
        kb = k_ref[pl.ds(ks, t), :]
        vb = v_ref[pl.ds(ks, t), :]
        if diagonal:
            mask = lax.broadcasted_iota(I32, (t, t), 1) < lax.broadcasted_iota(I32, (t, t), 0)
        cmin = None
        for h in range(nh):
            x = _dot_nt(qh[h], kb)
            sp = _softplus(x)
            if diagonal:
                sp = jnp.where(mask, sp, 0.0)
            suf = _dot2(sp, u)
            carry = carry_scr[h]
            w = jnp.exp(x - sp - suf[:, :t] - jnp.concatenate([carry] * (t // LANES), axis=1))
            if diagonal:
                w = jnp.where(mask, w, 0.0)
            acc_scr[h] += _dot(w.astype(BF16), vb)
            newc = carry + suf[:, t:]
            carry_scr[h] = newc
            m = jnp.min(newc)
            cmin = m if cmin is None else jnp.minimum(cmin, m)
        return cmin

    def cond(st):
        j, go = st
        return (j >= 0) & (go > 0)

    def body(st):
        j, _ = st
        return j - 1, (step(j, False) < SB_SKIP).astype(I32)

    lax.while_loop(cond, body, (i - 1, (step(i, True) < SB_SKIP).astype(I32)))
    o_ref[...] = jnp.where(lane < SB_HEAD_DIM, acc_scr[0], acc_scr[1]).astype(BF16)


def sb_attn(proj, u_mat, bsz, seq, qcol, kcol, vcol):
    t = SB_T
    nq = seq // t
    npairs = SB_HEADS * SB_HEAD_DIM // LANES
    nh = LANES // SB_HEAD_DIM
    return pl.pallas_call(
        _sb_kernel,
        grid=(bsz, npairs, nq),
        in_specs=[
            pl.BlockSpec((t, LANES), lambda b, p, i: (b * nq + i, qcol + p)),
            pl.BlockSpec((seq, LANES), lambda b, p, i: (b, kcol + p)),
            pl.BlockSpec((seq, LANES), lambda b, p, i: (b, vcol + p)),
            pl.BlockSpec((t, t + LANES), lambda b, p, i: (0, 0)),
        ],
        out_specs=pl.BlockSpec((t, LANES), lambda b, p, i: (b * nq + i, p)),
        out_shape=jax.ShapeDtypeStruct((bsz * seq, npairs * LANES), BF16),
        scratch_shapes=[pltpu.VMEM((nh, t, LANES), F32), pltpu.VMEM((nh, t, LANES), F32)],
        compiler_params=_params(("arbitrary", "arbitrary", "arbitrary")),
        name="sb_attn",
    )(proj, proj, proj, u_mat)


def _memkv_kernel(m_ref, g_ref, w_ref, o_ref):
    x = m_ref[...]
    n = x * lax.rsqrt(jnp.mean(x * x, axis=-1, keepdims=True) + EPS) * g_ref[...]
    o_ref[...] = _dot(n.astype(BF16), w_ref[...]).astype(BF16)


def mem_kv(mem2, g, w, mlen):
    rows, d = mem2.shape
    n = w.shape[1]
    return pl.pallas_call(
        _memkv_kernel,
        grid=(rows // mlen,),
        in_specs=[pl.BlockSpec((mlen, d), lambda b: (b, 0)),
                  pl.BlockSpec((1, d), lambda b: (0, 0)),
                  pl.BlockSpec((d, n), lambda b: (0, 0))],
        out_specs=pl.BlockSpec((mlen, n), lambda b: (b, 0)),
        out_shape=jax.ShapeDtypeStruct((rows, n), BF16),
        compiler_params=_params(("arbitrary",)),
        name="mem_kv",
    )(mem2, g, w)


def _merge_kernel(x_ref, yssd_ref, ysb_ref, gssd_ref, gsb_ref, wso_ref, wsbo_ref, wo_ref, gx_ref,
                  wxq_ref, kv_ref, wxo_ref, gm_ref, wrh_ref, wrl_ref, br_ref, tri_ref,
                  h2_ref, hn_ref, eidx_ref, rank_ref, gate_ref, cnt_ref, cnt_scr):
    @pl.when(pl.program_id(0) == 0)
    def _():
        cnt_scr[...] = jnp.zeros(cnt_scr.shape, F32)

    tm = x_ref.shape[0]
    y_ssd = _dot(yssd_ref[...], wso_ref[...])
    y_sb = _dot(ysb_ref[...], wsbo_ref[...])
    merged = (_sigmoid(gssd_ref[...].astype(F32)) * y_ssd
              + _sigmoid(gsb_ref[...].astype(F32)) * y_sb)
    h1 = x_ref[...] + _dot(merged.astype(BF16), wo_ref[...])

    n2 = h1 * lax.rsqrt(jnp.mean(h1 * h1, axis=-1, keepdims=True) + EPS) * gx_ref[...]
    q = (_dot(n2.astype(BF16), wxq_ref[...]) * (XA_HEAD_DIM ** -0.5)).astype(BF16)
    xi = XA_HEADS * XA_HEAD_DIM
    heads = []
    for h in range(XA_HEADS):
        kh = kv_ref[:, h * XA_HEAD_DIM:(h + 1) * XA_HEAD_DIM]
        vh = kv_ref[:, xi + h * XA_HEAD_DIM:xi + (h + 1) * XA_HEAD_DIM]
        lg = _dot_nt(q[:, h * XA_HEAD_DIM:(h + 1) * XA_HEAD_DIM], kh)
        ex = jnp.exp(lg - jnp.max(lg, axis=-1, keepdims=True))
        den = jnp.sum(ex, axis=-1, keepdims=True)
        heads.append(_dot(ex.astype(BF16), vh) / den)
    o = jnp.concatenate(heads, axis=1).astype(BF16)
    h2 = h1 + _dot(o, wxo_ref[...])
    h2_ref[...] = h2

    n3 = h2 * lax.rsqrt(jnp.mean(h2 * h2, axis=-1, keepdims=True) + EPS) * gm_ref[...]
    hn_ref[...] = n3

    nh, nl = _split_bf16(n3)
    lg = _dot(nh, wrh_ref[...]) + _dot(nl, wrh_ref[...]) + _dot(nh, wrl_ref[...]) + br_ref[...]
    lane = lax.broadcasted_iota(I32, (tm, LANES), 1)
    lanef = lane.astype(F32)
    work = lg
    vals, idxs = [], []
    for _ in range(TOP_K):
        m = jnp.max(work, axis=-1, keepdims=True)
        idx = jnp.min(jnp.where(work == m, lanef, float(LANES)), axis=-1, keepdims=True)
        vals.append(m)
        idxs.append(idx)
        work = jnp.where(lanef == idx, NEG_BIG * 2.0, work)
    exps = [jnp.exp(v - vals[0]) for v in vals]
    den = exps[0]
    for ex in exps[1:]:
        den = den + ex

    sel = jnp.zeros((tm, LANES), F32)
    for idx in idxs:
        sel = sel + (lanef == idx).astype(F32)
    incl = _dot(tri_ref[...], sel.astype(BF16))
    before = incl - sel + cnt_scr[...]
    eidx = jnp.zeros((tm, LANES), F32)
    rank = jnp.zeros((tm, LANES), F32)
    gate = jnp.zeros((tm, LANES), F32)
    for k in range(TOP_K):
        rk = jnp.sum(jnp.where(lanef == idxs[k], before, 0.0), axis=-1, keepdims=True)
        eidx = jnp.where(lane == k, idxs[k], eidx)
        rank = jnp.where(lane == k, rk, rank)
        gate = jnp.where(lane == k, exps[k] / den, gate)
    eidx_ref[...] = eidx.astype(I32)
    rank_ref[...] = rank.astype(I32)
    gate_ref[...] = gate
    cnt_scr[...] = cnt_scr[...] + incl[tm - 1:tm, :]
    cnt_ref[...] = cnt_scr[...].astype(I32)


def merge(x2, y_ssd, y_sb, proj, gcol, wso, wsbo, wo, gx, wxq, kv, wxo, gm, wrh, wrl, br, tri,
          seq, mlen, tm):
    t, d = x2.shape
    si = y_ssd.shape[1]
    sbi = y_sb.shape[1]
    xi2 = kv.shape[1]
    full = lambda a: pl.BlockSpec(a.shape, lambda i: (0,) * a.ndim)
    row = lambda w: pl.BlockSpec((tm, w), lambda i: (i, 0))
    return pl.pallas_call(
        _merge_kernel,
        grid=(t // tm,),
        in_specs=[
            row(d), row(si), row(sbi),
            pl.BlockSpec((tm, d), lambda i: (i, gcol)),
            pl.BlockSpec((tm, d), lambda i: (i, gcol + 1)),
            full(wso), full(wsbo), full(wo), full(gx), full(wxq),
            pl.BlockSpec((mlen, xi2), lambda i: ((i * tm) // seq, 0)),
            full(wxo), full(gm), full(wrh), full(wrl), full(br), full(tri),
        ],
        out_specs=[
            row(d), row(d),
            row(LANES), row(LANES), row(LANES),
            pl.BlockSpec((1, LANES), lambda i: (0, 0)),
        ],
        out_shape=[
            jax.ShapeDtypeStruct((t, d), F32),
            jax.ShapeDtypeStruct((t, d), F32),
            jax.ShapeDtypeStruct((t, LANES), I32),
            jax.ShapeDtypeStruct((t, LANES), I32),
            jax.ShapeDtypeStruct((t, LANES), F32),
            jax.ShapeDtypeStruct((1, LANES), I32),
        ],
        scratch_shapes=[pltpu.VMEM((1, LANES), F32)],
        compiler_params=_params(("arbitrary",)),
        name="merge",
    )(x2, y_ssd, y_sb, proj, proj, wso, wsbo, wo, gx, wxq, kv, wxo, gm, wrh, wrl, br, tri)


def _row(ref, r):
    return ref.at[pl.ds(r, 1), :]


def _dispatch_kernel(ps_ref, lo_ref, hi_ref, e_hbm, r_hbm, hn_ref, xin_out, e_sm, r_sm, zero_scr,
                     isem, sem, zsem):
    tm = hn_ref.shape[0]
    n = tm * TOP_K
    i = pl.program_id(0)
    ce = pltpu.make_async_copy(e_hbm.at[pl.ds(i * n, n)], e_sm, isem.at[0])
    cr = pltpu.make_async_copy(r_hbm.at[pl.ds(i * n, n)], r_sm, isem.at[1])
    ce.start()
    cr.start()
    ce.wait()
    cr.wait()

    def issue(tok, c):
        for k in range(TOP_K):
            a = tok * TOP_K + k
            dst = ps_ref[e_sm[a]] + r_sm[a]
            pltpu.make_async_copy(_row(hn_ref, tok), _row(xin_out, dst), sem).start()
        return c

    lax.fori_loop(0, tm, issue, 0)

    @pl.when(i == pl.num_programs(0) - 1)
    def _():
        zero_scr[...] = jnp.zeros(zero_scr.shape, F32)
        zrow = _row(zero_scr, 0)

        def per_span(s, c):
            def zissue(r, c2):
                pltpu.make_async_copy(zrow, _row(xin_out, r), zsem).start()
                return c2

            def zdrain(r, c2):
                pltpu.make_async_copy(zrow, _row(xin_out, r), zsem).wait()
                return c2

            lax.fori_loop(lo_ref[s], hi_ref[s], zissue, 0)
            lax.fori_loop(lo_ref[s], hi_ref[s], zdrain, 0)
            return c

        lax.fori_loop(0, lo_ref.shape[0], per_span, 0)

    for k in range(TOP_K):
        pltpu.make_async_copy(hn_ref, xin_out.at[pl.ds(0, tm), :], sem).wait()


def dispatch(pstart, pad_lo, pad_hi, e_flat, r_flat, hn, n_rows, tm=256):
    t, d = hn.shape
    grid_spec = pltpu.PrefetchScalarGridSpec(
        num_scalar_prefetch=3,
        grid=(t // tm,),
        in_specs=[
            pl.BlockSpec(memory_space=pl.ANY),
            pl.BlockSpec(memory_space=pl.ANY),
            pl.BlockSpec((tm, d), lambda i, ps, lo, hi: (i, 0)),
        ],
        out_specs=pl.BlockSpec(memory_space=pl.ANY),
        scratch_shapes=[pltpu.SMEM((tm * TOP_K,), I32), pltpu.SMEM((tm * TOP_K,), I32),
                        pltpu.VMEM((SUBLANES, d), F32),
                        pltpu.SemaphoreType.DMA((2,)), pltpu.SemaphoreType.DMA, pltpu.SemaphoreType.DMA],
    )
    return pl.pallas_call(
        _dispatch_kernel,
        grid_spec=grid_spec,
        out_shape=jax.ShapeDtypeStruct((n_rows, d), F32),
        compiler_params=_params(("arbitrary",)),
        name="dispatch",
    )(pstart, pad_lo, pad_hi, e_flat, r_flat, hn)


def _moe_kernel(be_ref, nu_ref, x_ref, wgu_ref, bgu_ref, wd_ref, bd_ref, o_ref, wgu_scr, wd_scr):
    i = pl.program_id(0)
    dff = wd_ref.shape[1]
    active = i < nu_ref[0]

    @pl.when(active & ((i == 0) | (be_ref[i] != be_ref[jnp.maximum(i - 1, 0)])))
    def _():
        for r in range(0, wgu_ref.shape[1], MOE_CAST_ROWS):
            wgu_scr[r:r + MOE_CAST_ROWS, :] = wgu_ref[0, r:r + MOE_CAST_ROWS, :].astype(BF16)
        for r in range(0, dff, MOE_CAST_ROWS):
            wd_scr[r:r + MOE_CAST_ROWS, :] = wd_ref[0, r:r + MOE_CAST_ROWS, :].astype(BF16)

    @pl.when(active)
    def _():
        gu = _dot(x_ref[...].astype(BF16), wgu_scr[...]) + bgu_ref[0]
        g_lin = jnp.minimum(gu[:, :dff], SWIGLU_LIMIT)
        u_lin = jnp.clip(gu[:, dff:], -SWIGLU_LIMIT, SWIGLU_LIMIT)
        act = g_lin * _sigmoid(SWIGLU_ALPHA * g_lin) * (u_lin + 1.0)
        o_ref[...] = _dot(act.astype(BF16), wd_scr[...]) + bd_ref[0]

    @pl.when(jnp.logical_not(active))
    def _():
        o_ref[...] = jnp.zeros(o_ref.shape, F32)


def moe_ffn(blk_expert, n_used, xin, wgu, bgu, wd, bd):
    n_rows, d = xin.shape
    nb = n_rows // MOE_BLOCK
    f2 = wgu.shape[2]
    dff = wd.shape[1]
    grid_spec = pltpu.PrefetchScalarGridSpec(
        num_scalar_prefetch=2,
        grid=(nb,),
        in_specs=[
            pl.BlockSpec((MOE_BLOCK, d), lambda i, be, nu: (jnp.minimum(i, nu[0] - 1), 0)),
            pl.BlockSpec((1, d, f2), lambda i, be, nu: (be[i], 0, 0)),
            pl.BlockSpec((1, 1, f2), lambda i, be, nu: (be[i], 0, 0)),
            pl.BlockSpec((1, dff, d), lambda i, be, nu: (be[i], 0, 0)),
            pl.BlockSpec((1, 1, d), lambda i, be, nu: (be[i], 0, 0)),
        ],
        out_specs=pl.BlockSpec((MOE_BLOCK, d), lambda i, be, nu: (i, 0)),
        scratch_shapes=[pltpu.VMEM((d, f2), BF16), pltpu.VMEM((dff, d), BF16)],
    )
    return pl.pallas_call(
        _moe_kernel,
        grid_spec=grid_spec,
        out_shape=jax.ShapeDtypeStruct(xin.shape, F32),
        compiler_params=_params(("arbitrary",)),
        name="moe_ffn",
    )(blk_expert, n_used, xin, wgu, bgu, wd, bd)


def _combine_kernel(ps_ref, e_hbm, r_hbm, yb_hbm, h2_ref, gate_ref, gf_ref, o_ref,
                    e_sm, r_sm, buf, isem, sem):
    tm = h2_ref.shape[0]
    n = tm * TOP_K
    i = pl.program_id(0)
    ce = pltpu.make_async_copy(e_hbm.at[pl.ds(i * n, n)], e_sm, isem.at[0])
    cr = pltpu.make_async_copy(r_hbm.at[pl.ds(i * n, n)], r_sm, isem.at[1])
    ce.start()
    cr.start()
    ce.wait()
    cr.wait()

    def issue(tok, c):
        for k in range(TOP_K):
            a = tok * TOP_K + k
            src = ps_ref[e_sm[a]] + r_sm[a]
            pltpu.make_async_copy(_row(yb_hbm, src), _row(buf.at[k], tok), sem).start()
        return c

    lax.fori_loop(0, tm, issue, 0)

    for k in range(TOP_K):
        pltpu.make_async_copy(yb_hbm.at[pl.ds(0, tm), :], buf.at[k], sem).wait()

    gate = gate_ref[...]
    h3 = h2_ref[...]
    for k in range(TOP_K):
        h3 = h3 + gate[:, k:k + 1] * buf[k]
    o_ref[...] = h3 * lax.rsqrt(jnp.mean(h3 * h3, axis=-1, keepdims=True) + EPS) * gf_ref[...]


def combine(pstart, e_flat, r_flat, yb, h2, gate, gf, tm=256):
    t, d = h2.shape
    grid_spec = pltpu.PrefetchScalarGridSpec(
        num_scalar_prefetch=1,
        grid=(t // tm,),
        in_specs=[
            pl.BlockSpec(memory_space=pl.ANY),
            pl.BlockSpec(memory_space=pl.ANY),
            pl.BlockSpec(memory_space=pl.ANY),
            pl.BlockSpec((tm, d), lambda i, ps: (i, 0)),
            pl.BlockSpec((tm, LANES), lambda i, ps: (i, 0)),
            pl.BlockSpec((1, d), lambda i, ps: (0, 0)),
        ],
        out_specs=pl.BlockSpec((tm, d), lambda i, ps: (i, 0)),
        scratch_shapes=[pltpu.SMEM((tm * TOP_K,), I32), pltpu.SMEM((tm * TOP_K,), I32),
                        pltpu.VMEM((TOP_K, tm, d), F32),
                        pltpu.SemaphoreType.DMA((2,)), pltpu.SemaphoreType.DMA],
    )
    return pl.pallas_call(
        _combine_kernel,
        grid_spec=grid_spec,
        out_shape=jax.ShapeDtypeStruct((t, d), F32),
        compiler_params=_params(("arbitrary",)),
        name="combine",
    )(pstart, e_flat, r_flat, yb, h2, gate, gf)


def _pad_lanes(v, fill=0.0):
    return jnp.pad(v.reshape(1, -1).astype(F32), ((0, 0), (0, LANES - v.shape[-1])), constant_values=fill)


def _layer(h, mem, p, merge_tm=512):
    bsz, seq, d = h.shape
    t = bsz * seq
    mlen = mem.shape[1]
    inner = SSD_HEADS * SSD_HEAD_DIM
    conv_ch = inner + 2 * SSD_GROUPS * SSD_STATE
    sbi = SB_HEADS * SB_HEAD_DIM
    n_experts = p["w_router"].shape[1]
    x2 = h.reshape(t, d)

    w_in = p["w_in"]
    dt0 = inner + conv_ch
    w_main = jnp.concatenate([w_in[:, :dt0], w_in[:, dt0 + SSD_HEADS:]], axis=1).astype(BF16)
    w_dt = jnp.pad(w_in[:, dt0:dt0 + SSD_HEADS], ((0, 0), (0, LANES - SSD_HEADS))).astype(BF16)
    proj, dt_raw = inproj(x2, p["g_mix"].reshape(1, d), w_main, w_dt)
    qcol = dt0 // LANES
    kcol = qcol + sbi // LANES
    vcol = kcol + sbi // LANES
    gcol = (dt0 + 3 * sbi) // d

    heads = jnp.arange(LANES)[:, None]
    chan = jnp.arange(inner)[None, :]
    e_mat = (chan // SSD_HEAD_DIM == heads).astype(BF16)
    y_ssd = ssd(proj, dt_raw, p["conv_w"], p["conv_b"].reshape(1, conv_ch), _pad_lanes(p["dt_bias"]),
                _pad_lanes(p["a_log"]), jnp.repeat(p["d_skip"], SSD_HEAD_DIM).reshape(1, inner),
                p["g_ssd"].reshape(1, inner), e_mat, bsz, seq)

    jj = jnp.arange(SB_T)[:, None]
    ss = jnp.arange(SB_T + LANES)[None, :]
    u_mat = ((ss >= SB_T) | (jj > ss)).astype(BF16)
    y_sb = sb_attn(proj, u_mat, bsz, seq, qcol, kcol, vcol)

    kv = mem_kv(mem.reshape(bsz * mlen, d), p["g_mem"].reshape(1, d), p["w_xkv"].astype(BF16), mlen)

    wr = jnp.pad(p["w_router"], ((0, 0), (0, LANES - n_experts)))
    wrh = wr.astype(BF16)
    wrl = (wr - wrh.astype(F32)).astype(BF16)
    tri = (jnp.arange(merge_tm)[None, :] <= jnp.arange(merge_tm)[:, None]).astype(BF16)
    h2, hn, eidx, rank, gate, counts = merge(
        x2, y_ssd, y_sb, proj, gcol, p["w_ssd_out"].astype(BF16), p["w_sb_out"].astype(BF16),
        p["w_out"].astype(BF16), p["g_xattn"].reshape(1, d), p["w_xq"].astype(BF16), kv,
        p["w_xo"].astype(BF16), p["g_moe"].reshape(1, d), wrh, wrl,
        _pad_lanes(p["b_router"], NEG_BIG), tri, seq, mlen, merge_tm)

    cnt = counts[0, :n_experts]
    padded = (cnt + MOE_BLOCK - 1) // MOE_BLOCK * MOE_BLOCK
    pend = jnp.cumsum(padded)
    pstart = (pend - padded).astype(I32)
    n_blocks = -(-(t * TOP_K) // MOE_BLOCK) + n_experts
    n_rows = n_blocks * MOE_BLOCK
    n_used = (pend[-1] // MOE_BLOCK).astype(I32)
    blk = jnp.minimum(jnp.arange(n_blocks, dtype=I32), n_used - 1) * MOE_BLOCK
    blk_expert = jnp.sum(blk[:, None] >= pend[None, :n_experts - 1], axis=1).astype(I32)
    e_flat = eidx[:, :TOP_K].reshape(-1)
    r_flat = rank[:, :TOP_K].reshape(-1)
    pad_lo = jnp.concatenate([pstart + cnt, pend[-1:]]).astype(I32)
    pad_hi = jnp.concatenate([pend, jnp.full((1,), n_rows)]).astype(I32)

    xin = dispatch(pstart, pad_lo, pad_hi, e_flat, r_flat, hn, n_rows)
    yb = moe_ffn(blk_expert, n_used.reshape(1), xin, p["w_gu"], p["b_gu"][:, None, :],
                 p["w_down"], p["b_down"][:, None, :])
    return h2, (pstart, e_flat, r_flat, yb, gate)


def kernel(x, mem, g_mix, w_in, conv_w, conv_b, dt_bias, a_log, d_skip, g_ssd, w_ssd_out, w_sb_out,
           w_out, g_xattn, g_mem, w_xq, w_xkv, w_xo, g_moe, w_router, b_router, w_gu, b_gu, w_down,
           b_down, g_final):
    stacked = dict(g_mix=g_mix, w_in=w_in, conv_w=conv_w, conv_b=conv_b, dt_bias=dt_bias, a_log=a_log,
                   d_skip=d_skip, g_ssd=g_ssd, w_ssd_out=w_ssd_out, w_sb_out=w_sb_out, w_out=w_out,
                   g_xattn=g_xattn, g_mem=g_mem, w_xq=w_xq, w_xkv=w_xkv, w_xo=w_xo, g_moe=g_moe,
                   w_router=w_router, b_router=b_router, w_gu=w_gu, b_gu=b_gu, w_down=w_down,
                   b_down=b_down)
    assert g_mix.shape[0] == 1, "stacked depth other than 1 is not supported"
    bsz, seq, d = x.shape
    p = {k: v[0] for k, v in stacked.items()}
    h2, (pstart, e_flat, r_flat, yb, gate) = _layer(x, mem, p)
    out = combine(pstart, e_flat, r_flat, yb, h2, gate, g_final.reshape(1, d))
    return out.reshape(bsz, seq, d)
```

```python
import functools

import jax
import jax.numpy as jnp
from jax import lax
from jax.experimental import pallas as pl
from jax.experimental.pallas import tpu as pltpu

F32 = jnp.float32
BF16 = jnp.bfloat16
I32 = jnp.int32

LANES = 128
SUBLANES = 8
EPS = 1e-6
VMEM_LIMIT = 56 * 1024 * 1024

SSD_HEADS = 16
SSD_HEAD_DIM = 64
SSD_GROUPS = 2
SSD_STATE = 128
SSD_CONV = 4
SB_HEADS = 8
SB_HEAD_DIM = 64
XA_HEADS = 4
XA_HEAD_DIM = 128
TOP_K = 4
SWIGLU_LIMIT = 7.0
SWIGLU_ALPHA = 1.702

SSD_L = 128
SB_T = 256
SB_SKIP = 110.0
MOE_BLOCK = 512
MOE_CAST_ROWS = 128
ROW_DMA_UNROLL = 2
NEG_BIG = -1e30


def _dot(a, b):
    return jnp.dot(a, b, preferred_element_type=F32)


def _dot_nt(a, b):
    return lax.dot_general(a, b, (((1,), (1,)), ((), ())), preferred_element_type=F32)


def _split_bf16(v):
    hi = v.astype(BF16)
    lo = (v - hi.astype(F32)).astype(BF16)
    return hi, lo


def _dot2(v, m):
    hi, lo = _split_bf16(v)
    return _dot(hi, m) + _dot(lo, m)


def _softplus(x):
    return jnp.maximum(x, 0.0) + jnp.log(1.0 + jnp.exp(-jnp.abs(x)))


def _sigmoid(x):
    return 1.0 / (1.0 + jnp.exp(-x))


def _params(sem, vmem=VMEM_LIMIT):
    return pltpu.CompilerParams(dimension_semantics=sem, vmem_limit_bytes=vmem)


def _store_rows(ref, val):
    n = val.shape[0]
    for s in range(SUBLANES):
        ref[pl.ds(s, n, stride=SUBLANES), :] = val[:, s * LANES:(s + 1) * LANES]


def _load_rows(ref, n):
    return jnp.concatenate([ref[pl.ds(s, n, stride=SUBLANES), :] for s in range(SUBLANES)], axis=1)


def _row(ref, r):
    return ref.at[pl.ds(pl.multiple_of(r * SUBLANES, SUBLANES), SUBLANES), :]


def _inproj_kernel(x_ref, g_ref, w_ref, wdt_ref, proj_ref, dt_ref, n_scr):
    @pl.when(pl.program_id(1) == 0)
    def _():
        x = x_ref[...]
        n = x * lax.rsqrt(jnp.mean(x * x, axis=-1, keepdims=True) + EPS) * g_ref[...]
        nb = n.astype(BF16)
        n_scr[...] = nb
        dt_ref[...] = _dot(nb, wdt_ref[...])

    proj_ref[...] = _dot(n_scr[...], w_ref[...]).astype(BF16)


def inproj(x2, g, w, wdt, tm=1024, tn=1024):
    t, d = x2.shape
    n = w.shape[1]
    return pl.pallas_call(
        _inproj_kernel,
        grid=(t // tm, n // tn),
        in_specs=[
            pl.BlockSpec((tm, d), lambda i, j: (i, 0)),
            pl.BlockSpec((1, d), lambda i, j: (0, 0)),
            pl.BlockSpec((d, tn), lambda i, j: (0, j)),
            pl.BlockSpec((d, LANES), lambda i, j: (0, 0)),
        ],
        out_specs=[
            pl.BlockSpec((tm, tn), lambda i, j: (i, j)),
            pl.BlockSpec((tm, LANES), lambda i, j: (i, 0)),
        ],
        out_shape=[
            jax.ShapeDtypeStruct((t, n), BF16),
            jax.ShapeDtypeStruct((t, LANES), F32),
        ],
        scratch_shapes=[pltpu.VMEM((tm, d), BF16)],
        compiler_params=_params(("arbitrary", "arbitrary")),
        name="inproj",
    )(x2, g, w, wdt)


def _ssd_kernel(z_ref, xs_ref, bc_ref, dt_ref, cw_ref, cb_ref, dtb_ref, alog_ref, dskip_ref,
                gn_ref, e_ref, o_ref, ext_scr, state_scr):
    L = SSD_L
    inner = SSD_HEADS * SSD_HEAD_DIM
    gw = inner // SSD_GROUPS
    n = SSD_STATE

    @pl.when(pl.program_id(1) == 0)
    def _():
        ext_scr[0:SUBLANES, :] = jnp.zeros((SUBLANES, ext_scr.shape[1]), F32)
        state_scr[...] = jnp.zeros(state_scr.shape, F32)

    ext_scr[SUBLANES:SUBLANES + L, 0:inner] = xs_ref[...].astype(F32)
    ext_scr[SUBLANES:SUBLANES + L, inner:] = bc_ref[...].astype(F32)
    conv = cb_ref[...]
    for k in range(SSD_CONV):
        off = SUBLANES - (SSD_CONV - 1) + k
        conv = conv + cw_ref[k:k + 1, :] * ext_scr[off:off + L, :]
    ext_scr[0:SUBLANES, :] = ext_scr[L:L + SUBLANES, :]
    xbc = conv * _sigmoid(conv)
    xs = xbc[:, :inner]

    dt = _softplus(dt_ref[...] + dtb_ref[...])
    da = dt * (-jnp.exp(alog_ref[...]))
    row = lax.broadcasted_iota(I32, (L, L), 0)
    col = lax.broadcasted_iota(I32, (L, L), 1)
    tril = col <= row
    cs = jnp.dot(tril.astype(F32), da, precision=lax.Precision.HIGHEST,
                 preferred_element_type=F32)
    cst = cs.T
    cs_last = cs[L - 1:L, :]

    e = e_ref[...]
    xdt = xs * _dot2(dt, e)
    xw = (xdt * _dot2(jnp.exp(cs_last - cs), e)).astype(BF16)
    od_e = _dot2(jnp.exp(cs), e)
    xdt_b = xdt.astype(BF16)
    lane = lax.broadcasted_iota(I32, (L, LANES), 1)

    y_parts = []
    new_states = []
    for g in range(SSD_GROUPS):
        bg = xbc[:, inner + g * n:inner + (g + 1) * n]
        cg = xbc[:, inner + SSD_GROUPS * n + g * n:inner + SSD_GROUPS * n + (g + 1) * n]
        bg_b = bg.astype(BF16)
        cg_b = cg.astype(BF16)
        cbm = _dot_nt(cg_b, bg_b)
        st = state_scr[:, g * gw:(g + 1) * gw]
        y_off = _dot(cg_b, st.astype(BF16))
        new_states.append(_dot(bg.T.astype(BF16), xw[:, g * gw:(g + 1) * gw]))
        heads_per_group = SSD_HEADS // SSD_GROUPS
        for hp in range(heads_per_group // 2):
            pair = xdt_b[:, g * gw + hp * LANES:g * gw + (hp + 1) * LANES]
            yd = []
            for sub in range(2):
                h = g * heads_per_group + hp * 2 + sub
                seg = cs[:, h:h + 1] - cst[h:h + 1, :]
                lm = jnp.exp(jnp.where(tril, seg, -jnp.inf))
                yd.append(_dot((cbm * lm).astype(BF16), pair))
            y_parts.append((jnp.where(lane < SSD_HEAD_DIM, yd[0], yd[1]), g, hp, y_off))

    cols = []
    for (yd, g, hp, y_off) in y_parts:
        c0 = g * gw + hp * LANES
        cols.append(yd + y_off[:, hp * LANES:(hp + 1) * LANES] * od_e[:, c0:c0 + LANES])
    y = jnp.concatenate(cols, axis=1) + xs * dskip_ref[...]
    zf = z_ref[...].astype(F32)
    y = y * (zf * _sigmoid(zf))
    outs = []
    for g in range(SSD_GROUPS):
        yg = y[:, g * gw:(g + 1) * gw]
        ms = jnp.mean(yg * yg, axis=-1, keepdims=True)
        outs.append(yg * lax.rsqrt(ms + EPS) * gn_ref[:, g * gw:(g + 1) * gw])
    o_ref[...] = jnp.concatenate(outs, axis=1).astype(BF16)

    cd_e = od_e[L - 1:L, :]
    for g in range(SSD_GROUPS):
        state_scr[:, g * gw:(g + 1) * gw] = (
            state_scr[:, g * gw:(g + 1) * gw] * cd_e[:, g * gw:(g + 1) * gw] + new_states[g])


def ssd(proj, dt_raw, conv_w, conv_b, dtb, alog, dskip_e, gn, e_mat, bsz, seq):
    L = SSD_L
    inner = SSD_HEADS * SSD_HEAD_DIM
    bcw = 2 * SSD_GROUPS * SSD_STATE
    nc = seq // L
    ch = inner + bcw
    full = lambda shape: pl.BlockSpec(shape, lambda b, c: (0,) * len(shape))
    return pl.pallas_call(
        _ssd_kernel,
        grid=(bsz, nc),
        in_specs=[
            pl.BlockSpec((L, inner), lambda b, c: (b * nc + c, 0)),
            pl.BlockSpec((L, inner), lambda b, c: (b * nc + c, 1)),
            pl.BlockSpec((L, bcw), lambda b, c: (b * nc + c, 2 * inner // bcw)),
            pl.BlockSpec((L, LANES), lambda b, c: (b * nc + c, 0)),
            full((SSD_CONV, ch)), full((1, ch)), full((1, LANES)), full((1, LANES)),
            full((1, inner)), full((1, inner)), full((LANES, inner)),
        ],
        out_specs=pl.BlockSpec((L, inner), lambda b, c: (b * nc + c, 0)),
        out_shape=jax.ShapeDtypeStruct((bsz * seq, inner), BF16),
        scratch_shapes=[pltpu.VMEM((L + 2 * SUBLANES, ch), F32), pltpu.VMEM((SSD_STATE, inner), F32)],
        compiler_params=_params(("arbitrary", "arbitrary")),
        name="ssd",
    )(proj, proj, proj, dt_raw, conv_w, conv_b, dtb, alog, dskip_e, gn, e_mat)


def _sb_kernel(q_ref, k_ref, v_ref, u_ref, o_ref, carry_scr, acc_scr):
    t = SB_T
    nh = LANES // SB_HEAD_DIM
    i = pl.program_id(2)
    lane = lax.broadcasted_iota(I32, (t, LANES), 1)
    q = q_ref[...]
    scale = SB_HEAD_DIM ** -0.5
    qh = []
    for h in range(nh):
        in_head = (lane >= h * SB_HEAD_DIM) & (lane < (h + 1) * SB_HEAD_DIM)
        qh.append((jnp.where(in_head, q, jnp.zeros_like(q)).astype(F32) * scale).astype(BF16))
    u = u_ref[...]
    carry_scr[...] = jnp.zeros(carry_scr.shape, F32)
    acc_scr[...] = jnp.zeros(acc_scr.shape, F32)

    def step(j, diagonal):
        ks = pl.multiple_of(j * t, t)
        kb = k_ref[pl.ds(ks, t), :]
        vb = v_ref[pl.ds(ks, t), :]
        if diagonal:
            mask = lax.broadcasted_iota(I32, (t, t), 1) < lax.broadcasted_iota(I32, (t, t), 0)
        cmin = None
        for h in range(nh):
            x = _dot_nt(qh[h], kb)
            sp = _softplus(x)
            if diagonal:
                sp = jnp.where(mask, sp, 0.0)
            suf = _dot2(sp, u)
            carry = carry_scr[h]
            w = jnp.exp(x - sp - suf[:, :t] - jnp.concatenate([carry] * (t // LANES), axis=1))
            if diagonal:
                w = jnp.where(mask, w, 0.0)
            acc_scr[h] += _dot(w.astype(BF16), vb)
            newc = carry + suf[:, t:]
            carry_scr[h] = newc
            m = jnp.min(newc)
            cmin = m if cmin is None else jnp.minimum(cmin, m)
        return cmin

    def cond(st):
        j, go = st
        return (j >= 0) & (go > 0)

    def body(st):
        j, _ = st
        return j - 1, (step(j, False) < SB_SKIP).astype(I32)

    lax.while_loop(cond, body, (i - 1, (step(i, True) < SB_SKIP).astype(I32)))
    o_ref[...] = jnp.where(lane < SB_HEAD_DIM, acc_scr[0], acc_scr[1]).astype(BF16)


def sb_attn(proj, u_mat, bsz, seq, qcol, kcol, vcol):
    t = SB_T
    nq = seq // t
    npairs = SB_HEADS * SB_HEAD_DIM // LANES
    nh = LANES // SB_HEAD_DIM
    return pl.pallas_call(
        _sb_kernel,
        grid=(bsz, npairs, nq),
        in_specs=[
            pl.BlockSpec((t, LANES), lambda b, p, i: (b * nq + i, qcol + p)),
            pl.BlockSpec((seq, LANES), lambda b, p, i: (b, kcol + p)),
            pl.BlockSpec((seq, LANES), lambda b, p, i: (b, vcol + p)),
            pl.BlockSpec((t, t + LANES), lambda b, p, i: (0, 0)),
        ],
        out_specs=pl.BlockSpec((t, LANES), lambda b, p, i: (b * nq + i, p)),
        out_shape=jax.ShapeDtypeStruct((bsz * seq, npairs * LANES), BF16),
        scratch_shapes=[pltpu.VMEM((nh, t, LANES), F32), pltpu.VMEM((nh, t, LANES), F32)],
        compiler_params=_params(("arbitrary", "arbitrary", "arbitrary")),
        name="sb_attn",
    )(proj, proj, proj, u_mat)


def _memkv_kernel(m_ref, g_ref, w_ref, o_ref):
    x = m_ref[...]
    n = x * lax.rsqrt(jnp.mean(x * x, axis=-1, keepdims=True) + EPS) * g_ref[...]
    o_ref[...] = _dot(n.astype(BF16), w_ref[...]).astype(BF16)


def mem_kv(mem2, g, w, mlen):
    rows, d = mem2.shape
    n = w.shape[1]
    return pl.pallas_call(
        _memkv_kernel,
        grid=(rows // mlen,),
        in_specs=[pl.BlockSpec((mlen, d), lambda b: (b, 0)),
                  pl.BlockSpec((1, d), lambda b: (0, 0)),
                  pl.BlockSpec((d, n), lambda b: (0, 0))],
        out_specs=pl.BlockSpec((mlen, n), lambda b: (b, 0)),
        out_shape=jax.ShapeDtypeStruct((rows, n), BF16),
        compiler_params=_params(("arbitrary",)),
        name="mem_kv",
    )(mem2, g, w)


def _merge_kernel(x_ref, yssd_ref, ysb_ref, gssd_ref, gsb_ref, wso_ref, wsbo_ref, wo_ref, gx_ref,
                  wxq_ref, kv_ref, wxo_ref, gm_ref, wrh_ref, wrl_ref, br_ref, tri_ref,
                  h2_ref, hn_ref, eidx_ref, rank_ref, gate_ref, cnt_ref, cnt_scr):
    @pl.when(pl.program_id(0) == 0)
    def _():
        cnt_scr[...] = jnp.zeros(cnt_scr.shape, F32)

    tm = x_ref.shape[0]
    y_ssd = _dot(yssd_ref[...], wso_ref[...])
    y_sb = _dot(ysb_ref[...], wsbo_ref[...])
    merged = (_sigmoid(gssd_ref[...].astype(F32)) * y_ssd
              + _sigmoid(gsb_ref[...].astype(F32)) * y_sb)
    h1 = x_ref[...] + _dot(merged.astype(BF16), wo_ref[...])

    n2 = h1 * lax.rsqrt(jnp.mean(h1 * h1, axis=-1, keepdims=True) + EPS) * gx_ref[...]
    q = (_dot(n2.astype(BF16), wxq_ref[...]) * (XA_HEAD_DIM ** -0.5)).astype(BF16)
    xi = XA_HEADS * XA_HEAD_DIM
    heads = []
    for h in range(XA_HEADS):
        kh = kv_ref[:, h * XA_HEAD_DIM:(h + 1) * XA_HEAD_DIM]
        vh = kv_ref[:, xi + h * XA_HEAD_DIM:xi + (h + 1) * XA_HEAD_DIM]
        lg = _dot_nt(q[:, h * XA_HEAD_DIM:(h + 1) * XA_HEAD_DIM], kh)
        ex = jnp.exp(lg - jnp.max(lg, axis=-1, keepdims=True))
        den = jnp.sum(ex, axis=-1, keepdims=True)
        heads.append(_dot(ex.astype(BF16), vh) / den)
    o = jnp.concatenate(heads, axis=1).astype(BF16)
    h2 = h1 + _dot(o, wxo_ref[...])
    h2_ref[...] = h2

    n3 = h2 * lax.rsqrt(jnp.mean(h2 * h2, axis=-1, keepdims=True) + EPS) * gm_ref[...]
    _store_rows(hn_ref, n3)

    nh, nl = _split_bf16(n3)
    lg = _dot(nh, wrh_ref[...]) + _dot(nl, wrh_ref[...]) + _dot(nh, wrl_ref[...]) + br_ref[...]
    lane = lax.broadcasted_iota(I32, (tm, LANES), 1)
    lanef = lane.astype(F32)
    work = lg
    vals, idxs = [], []
    for _ in range(TOP_K):
        m = jnp.max(work, axis=-1, keepdims=True)
        idx = jnp.min(jnp.where(work == m, lanef, float(LANES)), axis=-1, keepdims=True)
        vals.append(m)
        idxs.append(idx)
        work = jnp.where(lanef == idx, NEG_BIG * 2.0, work)
    exps = [jnp.exp(v - vals[0]) for v in vals]
    den = exps[0]
    for ex in exps[1:]:
        den = den + ex

    sel = jnp.zeros((tm, LANES), F32)
    for idx in idxs:
        sel = sel + (lanef == idx).astype(F32)
    incl = _dot(tri_ref[...], sel.astype(BF16))
    before = incl - sel + cnt_scr[...]
    eidx = jnp.zeros((tm, LANES), F32)
    rank = jnp.zeros((tm, LANES), F32)
    gate = jnp.zeros((tm, LANES), F32)
    for k in range(TOP_K):
        rk = jnp.sum(jnp.where(lanef == idxs[k], before, 0.0), axis=-1, keepdims=True)
        eidx = jnp.where(lane == k, idxs[k], eidx)
        rank = jnp.where(lane == k, rk, rank)
        gate = jnp.where(lane == k, exps[k] / den, gate)
    eidx_ref[...] = eidx.astype(I32)
    rank_ref[...] = rank.astype(I32)
    gate_ref[...] = gate
    cnt_scr[...] = cnt_scr[...] + incl[tm - 1:tm, :]
    cnt_ref[...] = cnt_scr[...].astype(I32)


def merge(x2, y_ssd, y_sb, proj, gcol, wso, wsbo, wo, gx, wxq, kv, wxo, gm, wrh, wrl, br, tri,
          seq, mlen, tm):
    t, d = x2.shape
    si = y_ssd.shape[1]
    sbi = y_sb.shape[1]
    xi2 = kv.shape[1]
    full = lambda a: pl.BlockSpec(a.shape, lambda i: (0,) * a.ndim)
    row = lambda w: pl.BlockSpec((tm, w), lambda i: (i, 0))
    return pl.pallas_call(
        _merge_kernel,
        grid=(t // tm,),
        in_specs=[
            row(d), row(si), row(sbi),
            pl.BlockSpec((tm, d), lambda i: (i, gcol)),
            pl.BlockSpec((tm, d), lambda i: (i, gcol + 1)),
            full(wso), full(wsbo), full(wo), full(gx), full(wxq),
            pl.BlockSpec((mlen, xi2), lambda i: ((i * tm) // seq, 0)),
            full(wxo), full(gm), full(wrh), full(wrl), full(br), full(tri),
        ],
        out_specs=[
            row(d),
            pl.BlockSpec((tm * d // LANES, LANES), lambda i: (i, 0)),
            row(LANES), row(LANES), row(LANES),
            pl.BlockSpec((1, LANES), lambda i: (0, 0)),
        ],
        out_shape=[
            jax.ShapeDtypeStruct((t, d), F32),
            jax.ShapeDtypeStruct((t * d // LANES, LANES), F32),
            jax.ShapeDtypeStruct((t, LANES), I32),
            jax.ShapeDtypeStruct((t, LANES), I32),
            jax.ShapeDtypeStruct((t, LANES), F32),
            jax.ShapeDtypeStruct((1, LANES), I32),
        ],
        scratch_shapes=[pltpu.VMEM((1, LANES), F32)],
        compiler_params=_params(("arbitrary",)),
        name="merge",
    )(x2, y_ssd, y_sb, proj, proj, wso, wsbo, wo, gx, wxq, kv, wxo, gm, wrh, wrl, br, tri)


def _index_copies(e_hbm, r_hbm, e_sm, r_sm, isem, step, slot):
    n = e_sm.shape[0] // 2
    return (pltpu.make_async_copy(e_hbm.at[pl.ds(step * n, n)], e_sm.at[pl.ds(slot * n, n)], isem.at[0, slot]),
            pltpu.make_async_copy(r_hbm.at[pl.ds(step * n, n)], r_sm.at[pl.ds(slot * n, n)], isem.at[1, slot]))


def _issue_rows(ps_ref, e_sm, r_sm, slot, tm, copy_for):
    n = tm * TOP_K

    def body(it, c):
        toks = [it * ROW_DMA_UNROLL + u for u in range(ROW_DMA_UNROLL)]
        rows = []
        for tok in toks:
            for k in range(TOP_K):
                a = slot * n + tok * TOP_K + k
                rows.append(ps_ref[e_sm[a]] + r_sm[a])
        for u, tok in enumerate(toks):
            for k in range(TOP_K):
                copy_for(tok, k, rows[u * TOP_K + k]).start(priority=k % 2)
        return c

    lax.fori_loop(0, tm // ROW_DMA_UNROLL, body, 0)


def _dispatch_kernel(ps_ref, lo_ref, hi_ref, e_hbm, r_hbm, hn_hbm, xin_out, e_sm, r_sm, zero_scr,
                     isem, sem, zsem):
    n = e_sm.shape[0] // 2
    tm = n // TOP_K
    i = pl.program_id(0)
    last = pl.num_programs(0) - 1
    slot = i % 2
    idx = functools.partial(_index_copies, e_hbm, r_hbm, e_sm, r_sm, isem)

    @pl.when(i == 0)
    def _():
        for c in idx(0, 0):
            c.start()

    for c in idx(i, slot):
        c.wait()

    @pl.when(i < last)
    def _():
        for c in idx(i + 1, 1 - slot):
            c.start()

    _issue_rows(ps_ref, e_sm, r_sm, slot, tm,
                lambda tok, k, row: pltpu.make_async_copy(_row(hn_hbm, i * tm + tok), _row(xin_out, row),
                                                          sem.at[slot]))

    def wait_rows(s):
        pltpu.make_async_copy(hn_hbm.at[pl.ds(0, n * SUBLANES), :],
                              xin_out.at[pl.ds(0, n * SUBLANES), :], sem.at[s]).wait()

    @pl.when(i > 0)
    def _():
        wait_rows(1 - slot)

    @pl.when(i == last)
    def _():
        zero_scr[...] = jnp.zeros(zero_scr.shape, F32)
        zrow = _row(zero_scr, 0)

        def per_span(s, c):
            def zissue(r, c2):
                pltpu.make_async_copy(zrow, _row(xin_out, r), zsem).start()
                return c2

            def zdrain(r, c2):
                pltpu.make_async_copy(zrow, _row(xin_out, r), zsem).wait()
                return c2

            lax.fori_loop(lo_ref[s], hi_ref[s], zissue, 0)
            lax.fori_loop(lo_ref[s], hi_ref[s], zdrain, 0)
            return c

        lax.fori_loop(0, lo_ref.shape[0], per_span, 0)
        wait_rows(slot)


def dispatch(pstart, pad_lo, pad_hi, e_flat, r_flat, hn, n_rows, tm=512):
    t = hn.shape[0] // SUBLANES
    grid_spec = pltpu.PrefetchScalarGridSpec(
        num_scalar_prefetch=3,
        grid=(t // tm,),
        in_specs=[pl.BlockSpec(memory_space=pl.ANY)] * 3,
        out_specs=pl.BlockSpec(memory_space=pl.ANY),
        scratch_shapes=[pltpu.SMEM((2 * tm * TOP_K,), I32), pltpu.SMEM((2 * tm * TOP_K,), I32),
                        pltpu.VMEM((SUBLANES, LANES), F32),
                        pltpu.SemaphoreType.DMA((2, 2)), pltpu.SemaphoreType.DMA((2,)),
                        pltpu.SemaphoreType.DMA],
    )
    return pl.pallas_call(
        _dispatch_kernel,
        grid_spec=grid_spec,
        out_shape=jax.ShapeDtypeStruct((n_rows * SUBLANES, LANES), F32),
        compiler_params=_params(("arbitrary",)),
        name="dispatch",
    )(pstart, pad_lo, pad_hi, e_flat, r_flat, hn)


def _moe_kernel(be_ref, nu_ref, x_ref, wgu_ref, bgu_ref, wd_ref, bd_ref, o_ref, wgu_scr, wd_scr):
    i = pl.program_id(0)
    dff = wd_ref.shape[1]
    active = i < nu_ref[0]

    @pl.when(active & ((i == 0) | (be_ref[i] != be_ref[jnp.maximum(i - 1, 0)])))
    def _():
        for r in range(0, wgu_ref.shape[1], MOE_CAST_ROWS):
            wgu_scr[r:r + MOE_CAST_ROWS, :] = wgu_ref[0, r:r + MOE_CAST_ROWS, :].astype(BF16)
        for r in range(0, dff, MOE_CAST_ROWS):
            wd_scr[r:r + MOE_CAST_ROWS, :] = wd_ref[0, r:r + MOE_CAST_ROWS, :].astype(BF16)

    @pl.when(active)
    def _():
        x = _load_rows(x_ref, MOE_BLOCK).astype(BF16)
        gu = _dot(x, wgu_scr[...]) + bgu_ref[0]
        g_lin = jnp.minimum(gu[:, :dff], SWIGLU_LIMIT)
        u_lin = jnp.clip(gu[:, dff:], -SWIGLU_LIMIT, SWIGLU_LIMIT)
        act = g_lin * _sigmoid(SWIGLU_ALPHA * g_lin) * (u_lin + 1.0)
        _store_rows(o_ref, _dot(act.astype(BF16), wd_scr[...]) + bd_ref[0])

    @pl.when(jnp.logical_not(active))
    def _():
        o_ref[...] = jnp.zeros(o_ref.shape, F32)


def moe_ffn(blk_expert, n_used, xin, wgu, bgu, wd, bd):
    blk_rows = MOE_BLOCK * SUBLANES
    nb = xin.shape[0] // blk_rows
    d = wgu.shape[1]
    f2 = wgu.shape[2]
    dff = wd.shape[1]
    grid_spec = pltpu.PrefetchScalarGridSpec(
        num_scalar_prefetch=2,
        grid=(nb,),
        in_specs=[
            pl.BlockSpec((blk_rows, LANES), lambda i, be, nu: (jnp.minimum(i, nu[0] - 1), 0)),
            pl.BlockSpec((1, d, f2), lambda i, be, nu: (be[i], 0, 0)),
            pl.BlockSpec((1, 1, f2), lambda i, be, nu: (be[i], 0, 0)),
            pl.BlockSpec((1, dff, d), lambda i, be, nu: (be[i], 0, 0)),
            pl.BlockSpec((1, 1, d), lambda i, be, nu: (be[i], 0, 0)),
        ],
        out_specs=pl.BlockSpec((blk_rows, LANES), lambda i, be, nu: (i, 0)),
        scratch_shapes=[pltpu.VMEM((d, f2), BF16), pltpu.VMEM((dff, d), BF16)],
    )
    return pl.pallas_call(
        _moe_kernel,
        grid_spec=grid_spec,
        out_shape=jax.ShapeDtypeStruct(xin.shape, F32),
        compiler_params=_params(("arbitrary",)),
        name="moe_ffn",
    )(blk_expert, n_used, xin, wgu, bgu, wd, bd)


def _combine_kernel(ps_ref, e_hbm, r_hbm, yb_hbm, h2_ref, gate_ref, gf_ref, o_ref,
                    e_sm, r_sm, buf, isem, sem):
    tm = h2_ref.shape[0]
    i = pl.program_id(0)
    last = pl.num_programs(0) - 1
    slot = i % 2
    idx = functools.partial(_index_copies, e_hbm, r_hbm, e_sm, r_sm, isem)

    def gather(s):
        _issue_rows(ps_ref, e_sm, r_sm, s, tm,
                    lambda tok, k, row: pltpu.make_async_copy(_row(yb_hbm, row), _row(buf.at[s, k], tok),
                                                              sem.at[s]))

    @pl.when(i == 0)
    def _():
        for c in idx(0, 0):
            c.start()
        for c in idx(0, 0):
            c.wait()
        gather(0)

        @pl.when(last > 0)
        def _():
            for c in idx(1, 1):
                c.start()

    @pl.when(i < last)
    def _():
        for c in idx(i + 1, 1 - slot):
            c.wait()
        gather(1 - slot)

    @pl.when(i + 2 <= last)
    def _():
        for c in idx(i + 2, slot):
            c.start()

    for k in range(TOP_K):
        pltpu.make_async_copy(yb_hbm.at[pl.ds(0, tm * SUBLANES), :], buf.at[slot, k], sem.at[slot]).wait()

    gate = gate_ref[...]
    h3 = h2_ref[...]
    for k in range(TOP_K):
        h3 = h3 + gate[:, k:k + 1] * _load_rows(buf.at[slot, k], tm)
    o_ref[...] = h3 * lax.rsqrt(jnp.mean(h3 * h3, axis=-1, keepdims=True) + EPS) * gf_ref[...]


def combine(pstart, e_flat, r_flat, yb, h2, gate, gf, tm=256):
    t, d = h2.shape
    grid_spec = pltpu.PrefetchScalarGridSpec(
        num_scalar_prefetch=1,
        grid=(t // tm,),
        in_specs=[
            pl.BlockSpec(memory_space=pl.ANY),
            pl.BlockSpec(memory_space=pl.ANY),
            pl.BlockSpec(memory_space=pl.ANY),
            pl.BlockSpec((tm, d), lambda i, ps: (i, 0)),
            pl.BlockSpec((tm, LANES), lambda i, ps: (i, 0)),
            pl.BlockSpec((1, d), lambda i, ps: (0, 0)),
        ],
        out_specs=pl.BlockSpec((tm, d), lambda i, ps: (i, 0)),
        scratch_shapes=[pltpu.SMEM((2 * tm * TOP_K,), I32), pltpu.SMEM((2 * tm * TOP_K,), I32),
                        pltpu.VMEM((2, TOP_K, tm * SUBLANES, LANES), F32),
                        pltpu.SemaphoreType.DMA((2, 2)), pltpu.SemaphoreType.DMA((2,))],
    )
    return pl.pallas_call(
        _combine_kernel,
        grid_spec=grid_spec,
        out_shape=jax.ShapeDtypeStruct((t, d), F32),
        compiler_params=_params(("arbitrary",)),
        name="combine",
    )(pstart, e_flat, r_flat, yb, h2, gate, gf)


def _pad_lanes(v, fill=0.0):
    return jnp.pad(v.reshape(1, -1).astype(F32), ((0, 0), (0, LANES - v.shape[-1])), constant_values=fill)


def _layer(h, mem, p, merge_tm=512):
    bsz, seq, d = h.shape
    assert d == SUBLANES * LANES, "the row-tile layout needs one (8,128) tile per model row"
    t = bsz * seq
    mlen = mem.shape[1]
    inner = SSD_HEADS * SSD_HEAD_DIM
    conv_ch = inner + 2 * SSD_GROUPS * SSD_STATE
    sbi = SB_HEADS * SB_HEAD_DIM
    n_experts = p["w_router"].shape[1]
    x2 = h.reshape(t, d)

    w_in = p["w_in"]
    dt0 = inner + conv_ch
    w_main = jnp.concatenate([w_in[:, :dt0], w_in[:, dt0 + SSD_HEADS:]], axis=1).astype(BF16)
    w_dt = jnp.pad(w_in[:, dt0:dt0 + SSD_HEADS], ((0, 0), (0, LANES - SSD_HEADS))).astype(BF16)
    proj, dt_raw = inproj(x2, p["g_mix"].reshape(1, d), w_main, w_dt)
    qcol = dt0 // LANES
    kcol = qcol + sbi // LANES
    vcol = kcol + sbi // LANES
    gcol = (dt0 + 3 * sbi) // d

    heads = jnp.arange(LANES)[:, None]
    chan = jnp.arange(inner)[None, :]
    e_mat = (chan // SSD_HEAD_DIM == heads).astype(BF16)
    y_ssd = ssd(proj, dt_raw, p["conv_w"], p["conv_b"].reshape(1, conv_ch), _pad_lanes(p["dt_bias"]),
                _pad_lanes(p["a_log"]), jnp.repeat(p["d_skip"], SSD_HEAD_DIM).reshape(1, inner),
                p["g_ssd"].reshape(1, inner), e_mat, bsz, seq)

    jj = jnp.arange(SB_T)[:, None]
    ss = jnp.arange(SB_T + LANES)[None, :]
    u_mat = ((ss >= SB_T) | (jj > ss)).astype(BF16)
    y_sb = sb_attn(proj, u_mat, bsz, seq, qcol, kcol, vcol)

    kv = mem_kv(mem.reshape(bsz * mlen, d), p["g_mem"].reshape(1, d), p["w_xkv"].astype(BF16), mlen)

    wr = jnp.pad(p["w_router"], ((0, 0), (0, LANES - n_experts)))
    wrh = wr.astype(BF16)
    wrl = (wr - wrh.astype(F32)).astype(BF16)
    tri = (jnp.arange(merge_tm)[None, :] <= jnp.arange(merge_tm)[:, None]).astype(BF16)
    h2, hn, eidx, rank, gate, counts = merge(
        x2, y_ssd, y_sb, proj, gcol, p["w_ssd_out"].astype(BF16), p["w_sb_out"].astype(BF16),
        p["w_out"].astype(BF16), p["g_xattn"].reshape(1, d), p["w_xq"].astype(BF16), kv,
        p["w_xo"].astype(BF16), p["g_moe"].reshape(1, d), wrh, wrl,
        _pad_lanes(p["b_router"], NEG_BIG), tri, seq, mlen, merge_tm)

    cnt = counts[0, :n_experts]
    padded = (cnt + MOE_BLOCK - 1) // MOE_BLOCK * MOE_BLOCK
    pend = jnp.cumsum(padded)
    pstart = (pend - padded).astype(I32)
    n_blocks = -(-(t * TOP_K) // MOE_BLOCK) + n_experts
    n_rows = n_blocks * MOE_BLOCK
    n_used = (pend[-1] // MOE_BLOCK).astype(I32)
    blk = jnp.minimum(jnp.arange(n_blocks, dtype=I32), n_used - 1) * MOE_BLOCK
    blk_expert = jnp.sum(blk[:, None] >= pend[None, :n_experts - 1], axis=1).astype(I32)
    e_flat = eidx[:, :TOP_K].reshape(-1)
    r_flat = rank[:, :TOP_K].reshape(-1)
    pad_lo = jnp.concatenate([pstart + cnt, pend[-1:]]).astype(I32)
    pad_hi = jnp.concatenate([pend, jnp.full((1,), n_rows)]).astype(I32)

    xin = dispatch(pstart, pad_lo, pad_hi, e_flat, r_flat, hn, n_rows)
    yb = moe_ffn(blk_expert, n_used.reshape(1), xin, p["w_gu"], p["b_gu"][:, None, :],
                 p["w_down"], p["b_down"][:, None, :])
    return h2, (pstart, e_flat, r_flat, yb, gate)


def kernel(x, mem, g_mix, w_in, conv_w, conv_b, dt_bias, a_log, d_skip, g_ssd, w_ssd_out, w_sb_out,
           w_out, g_xattn, g_mem, w_xq, w_xkv, w_xo, g_moe, w_router, b_router, w_gu, b_gu, w_down,
           b_down, g_final):
    stacked = dict(g_mix=g_mix, w_in=w_in, conv_w=conv_w, conv_b=conv_b, dt_bias=dt_bias, a_log=a_log,
                   d_skip=d_skip, g_ssd=g_ssd, w_ssd_out=w_ssd_out, w_sb_out=w_sb_out, w_out=w_out,
                   g_xattn=g_xattn, g_mem=g_mem, w_xq=w_xq, w_xkv=w_xkv, w_xo=w_xo, g_moe=g_moe,
                   w_router=w_router, b_router=b_router, w_gu=w_gu, b_gu=b_gu, w_down=w_down,
                   b_down=b_down)
    assert g_mix.shape[0] == 1, "stacked depth other than 1 is not supported"
    bsz, seq, d = x.shape
    p = {k: v[0] for k, v in stacked.items()}
    h2, (pstart, e_flat, r_flat, yb, gate) = _layer(x, mem, p)
    out = combine(pstart, e_flat, r_flat, yb, h2, gate, g_final.reshape(1, d))
    return out.reshape(bsz, seq, d)
```

```python
import functools

import jax
import jax.numpy as jnp
from jax import lax
from jax.experimental import pallas as pl
from jax.experimental.pallas import tpu as pltpu

F32 = jnp.float32
BF16 = jnp.bfloat16
I32 = jnp.int32

LANES = 128
SUBLANES = 8
EPS = 1e-6
VMEM_LIMIT = 56 * 1024 * 1024

SSD_HEADS = 16
SSD_HEAD_DIM = 64
SSD_GROUPS = 2
SSD_STATE = 128
SSD_CONV = 4
SB_HEADS = 8
SB_HEAD_DIM = 64
XA_HEADS = 4
XA_HEAD_DIM = 128
TOP_K = 4
SWIGLU_LIMIT = 7.0
SWIGLU_ALPHA = 1.702

SSD_L = 128
SB_T = 256
SB_SKIP = 110.0
MOE_BLOCK = 512
MOE_CAST_ROWS = 128
ROW_DMA_UNROLL = 2
NEG_BIG = -1e30


def _dot(a, b):
    return jnp.dot(a, b, preferred_element_type=F32)


def _dot_nt(a, b):
    return lax.dot_general(a, b, (((1,), (1,)), ((), ())), preferred_element_type=F32)


def _split_bf16(v):
    hi = v.astype(BF16)
    lo = (v - hi.astype(F32)).astype(BF16)
    return hi, lo


def _dot2(v, m):
    hi, lo = _split_bf16(v)
    return _dot(hi, m) + _dot(lo, m)


def _softplus(x):
    return jnp.maximum(x, 0.0) + jnp.log(1.0 + jnp.exp(-jnp.abs(x)))


def _sigmoid(x):
    return 1.0 / (1.0 + jnp.exp(-x))


def _params(sem, vmem=VMEM_LIMIT):
    return pltpu.CompilerParams(dimension_semantics=sem, vmem_limit_bytes=vmem)


def _store_rows(ref, val):
    n = val.shape[0]
    for s in range(SUBLANES):
        ref[pl.ds(s, n, stride=SUBLANES), :] = val[:, s * LANES:(s + 1) * LANES]


def _load_rows(ref, n):
    return jnp.concatenate([ref[pl.ds(s, n, stride=SUBLANES), :] for s in range(SUBLANES)], axis=1)


def _row(ref, r):
    return ref.at[pl.ds(pl.multiple_of(r * SUBLANES, SUBLANES), SUBLANES), :]


def _inproj_kernel(x_ref, g_ref, w_ref, wdt_ref, proj_ref, dt_ref, n_scr):
    @pl.when(pl.program_id(1) == 0)
    def _():
        x = x_ref[...]
        n = x * lax.rsqrt(jnp.mean(x * x, axis=-1, keepdims=True) + EPS) * g_ref[...]
        nb = n.astype(BF16)
        n_scr[...] = nb
        dt_ref[...] = _dot(nb, wdt_ref[...])

    proj_ref[...] = _dot(n_scr[...], w_ref[...]).astype(BF16)


def inproj(x2, g, w, wdt, tm=1024, tn=1024):
    t, d = x2.shape
    n = w.shape[1]
    return pl.pallas_call(
        _inproj_kernel,
        grid=(t // tm, n // tn),
        in_specs=[
            pl.BlockSpec((tm, d), lambda i, j: (i, 0)),
            pl.BlockSpec((1, d), lambda i, j: (0, 0)),
            pl.BlockSpec((d, tn), lambda i, j: (0, j)),
            pl.BlockSpec((d, LANES), lambda i, j: (0, 0)),
        ],
        out_specs=[
            pl.BlockSpec((tm, tn), lambda i, j: (i, j)),
            pl.BlockSpec((tm, LANES), lambda i, j: (i, 0)),
        ],
        out_shape=[
            jax.ShapeDtypeStruct((t, n), BF16),
            jax.ShapeDtypeStruct((t, LANES), F32),
        ],
        scratch_shapes=[pltpu.VMEM((tm, d), BF16)],
        compiler_params=_params(("arbitrary", "arbitrary")),
        name="inproj",
    )(x2, g, w, wdt)


def _ssd_kernel(z_ref, xs_ref, bc_ref, dt_ref, cw_ref, cb_ref, dtb_ref, alog_ref, dskip_ref,
                gn_ref, e_ref, o_ref, ext_scr, state_scr):
    L = SSD_L
    inner = SSD_HEADS * SSD_HEAD_DIM
    gw = inner // SSD_GROUPS
    n = SSD_STATE

    @pl.when(pl.program_id(1) == 0)
    def _():
        ext_scr[0:SUBLANES, :] = jnp.zeros((SUBLANES, ext_scr.shape[1]), F32)
        state_scr[...] = jnp.zeros(state_scr.shape, F32)

    ext_scr[SUBLANES:SUBLANES + L, 0:inner] = xs_ref[...].astype(F32)
    ext_scr[SUBLANES:SUBLANES + L, inner:] = bc_ref[...].astype(F32)
    conv = cb_ref[...]
    for k in range(SSD_CONV):
        off = SUBLANES - (SSD_CONV - 1) + k
        conv = conv + cw_ref[k:k + 1, :] * ext_scr[off:off + L, :]
    ext_scr[0:SUBLANES, :] = ext_scr[L:L + SUBLANES, :]
    xbc = conv * _sigmoid(conv)
    xs = xbc[:, :inner]

    dt = _softplus(dt_ref[...] + dtb_ref[...])
    da = dt * (-jnp.exp(alog_ref[...]))
    row = lax.broadcasted_iota(I32, (L, L), 0)
    col = lax.broadcasted_iota(I32, (L, L), 1)
    tril = col <= row
    cs = jnp.dot(tril.astype(F32), da, precision=lax.Precision.HIGHEST,
                 preferred_element_type=F32)
    cst = cs.T
    cs_last = cs[L - 1:L, :]

    e = e_ref[...]
    xdt = xs * _dot2(dt, e)
    xw = (xdt * _dot2(jnp.exp(cs_last - cs), e)).astype(BF16)
    od_e = _dot2(jnp.exp(cs), e)
    xdt_b = xdt.astype(BF16)
    lane = lax.broadcasted_iota(I32, (L, LANES), 1)

    y_parts = []
    new_states = []
    for g in range(SSD_GROUPS):
        bg = xbc[:, inner + g * n:inner + (g + 1) * n]
        cg = xbc[:, inner + SSD_GROUPS * n + g * n:inner + SSD_GROUPS * n + (g + 1) * n]
        bg_b = bg.astype(BF16)
        cg_b = cg.astype(BF16)
        cbm = _dot_nt(cg_b, bg_b)
        st = state_scr[:, g * gw:(g + 1) * gw]
        y_off = _dot(cg_b, st.astype(BF16))
        new_states.append(_dot(bg.T.astype(BF16), xw[:, g * gw:(g + 1) * gw]))
        heads_per_group = SSD_HEADS // SSD_GROUPS
        for hp in range(heads_per_group // 2):
            pair = xdt_b[:, g * gw + hp * LANES:g * gw + (hp + 1) * LANES]
            yd = []
            for sub in range(2):
                h = g * heads_per_group + hp * 2 + sub
                seg = cs[:, h:h + 1] - cst[h:h + 1, :]
                lm = jnp.exp(jnp.where(tril, seg, -jnp.inf))
                yd.append(_dot((cbm * lm).astype(BF16), pair))
            y_parts.append((jnp.where(lane < SSD_HEAD_DIM, yd[0], yd[1]), g, hp, y_off))

    cols = []
    for (yd, g, hp, y_off) in y_parts:
        c0 = g * gw + hp * LANES
        cols.append(yd + y_off[:, hp * LANES:(hp + 1) * LANES] * od_e[:, c0:c0 + LANES])
    y = jnp.concatenate(cols, axis=1) + xs * dskip_ref[...]
    zf = z_ref[...].astype(F32)
    y = y * (zf * _sigmoid(zf))
    outs = []
    for g in range(SSD_GROUPS):
        yg = y[:, g * gw:(g + 1) * gw]
        ms = jnp.mean(yg * yg, axis=-1, keepdims=True)
        outs.append(yg * lax.rsqrt(ms + EPS) * gn_ref[:, g * gw:(g + 1) * gw])
    o_ref[...] = jnp.concatenate(outs, axis=1).astype(BF16)

    cd_e = od_e[L - 1:L, :]
    for g in range(SSD_GROUPS):
        state_scr[:, g * gw:(g + 1) * gw] = (
            state_scr[:, g * gw:(g + 1) * gw] * cd_e[:, g * gw:(g + 1) * gw] + new_states[g])


def ssd(proj, dt_raw, conv_w, conv_b, dtb, alog, dskip_e, gn, e_mat, bsz, seq):
    L = SSD_L
    inner = SSD_HEADS * SSD_HEAD_DIM
    bcw = 2 * SSD_GROUPS * SSD_STATE
    nc = seq // L
    ch = inner + bcw
    full = lambda shape: pl.BlockSpec(shape, lambda b, c: (0,) * len(shape))
    return pl.pallas_call(
        _ssd_kernel,
        grid=(bsz, nc),
        in_specs=[
            pl.BlockSpec((L, inner), lambda b, c: (b * nc + c, 0)),
            pl.BlockSpec((L, inner), lambda b, c: (b * nc + c, 1)),
            pl.BlockSpec((L, bcw), lambda b, c: (b * nc + c, 2 * inner // bcw)),
            pl.BlockSpec((L, LANES), lambda b, c: (b * nc + c, 0)),
            full((SSD_CONV, ch)), full((1, ch)), full((1, LANES)), full((1, LANES)),
            full((1, inner)), full((1, inner)), full((LANES, inner)),
        ],
        out_specs=pl.BlockSpec((L, inner), lambda b, c: (b * nc + c, 0)),
        out_shape=jax.ShapeDtypeStruct((bsz * seq, inner), BF16),
        scratch_shapes=[pltpu.VMEM((L + 2 * SUBLANES, ch), F32), pltpu.VMEM((SSD_STATE, inner), F32)],
        compiler_params=_params(("arbitrary", "arbitrary")),
        name="ssd",
    )(proj, proj, proj, dt_raw, conv_w, conv_b, dtb, alog, dskip_e, gn, e_mat)


def _sb_kernel(q_ref, k_ref, v_ref, u_ref, o_ref, carry_scr, acc_scr):
    t = SB_T
    nh = LANES // SB_HEAD_DIM
    i = pl.program_id(2)
    lane = lax.broadcasted_iota(I32, (t, LANES), 1)
    q = q_ref[...]
    scale = SB_HEAD_DIM ** -0.5
    qh = []
    for h in range(nh):
        in_head = (lane >= h * SB_HEAD_DIM) & (lane < (h + 1) * SB_HEAD_DIM)
        qh.append((jnp.where(in_head, q, jnp.zeros_like(q)).astype(F32) * scale).astype(BF16))
    u = u_ref[...]

    def step(j, diagonal, state):
        ks = pl.multiple_of(j * t, t)
        kb = k_ref[pl.ds(ks, t), :]
        vb = v_ref[pl.ds(ks, t), :]
        if diagonal:
            mask = lax.broadcasted_iota(I32, (t, t), 1) < lax.broadcasted_iota(I32, (t, t), 0)
        out = []
        for h in range(nh):
            x = _dot_nt(qh[h], kb)
            sp = _softplus(x)
            if diagonal:
                sp = jnp.where(mask, sp, 0.0)
            suf = _dot2(sp, u)
            e = x - sp - suf[:, :t]
            if state is not None:
                e = e - jnp.concatenate([state[h][0]] * (t // LANES), axis=1)
            w = jnp.exp(e)
            if diagonal:
                w = jnp.where(mask, w, 0.0)
            pv = _dot(w.astype(BF16), vb)
            if state is None:
                out.append((suf[:, t:], pv))
            else:
                out.append((state[h][0] + suf[:, t:], state[h][1] + pv))
        return out

    def smallest(state):
        return functools.reduce(jnp.minimum, [jnp.min(c) for c, _ in state])

    def save(state):
        for h in range(nh):
            carry_scr[h] = state[h][0]
            acc_scr[h] = state[h][1]

    def cond(st):
        j, go = st
        return (j >= 0) & (go > 0)

    def body(st):
        j, _ = st
        state = step(j, False, [(carry_scr[h], acc_scr[h]) for h in range(nh)])
        save(state)
        return j - 1, (smallest(state) < SB_SKIP).astype(I32)

    @pl.when(i == 0)
    def _():
        save(step(0, True, None))

    @pl.when(i > 0)
    def _():
        state = step(i - 1, False, step(i, True, None))
        save(state)
        lax.while_loop(cond, body, (i - 2, (smallest(state) < SB_SKIP).astype(I32)))

    o_ref[...] = jnp.where(lane < SB_HEAD_DIM, acc_scr[0], acc_scr[1]).astype(BF16)


def sb_attn(proj, u_mat, bsz, seq, qcol, kcol, vcol):
    t = SB_T
    nq = seq // t
    npairs = SB_HEADS * SB_HEAD_DIM // LANES
    nh = LANES // SB_HEAD_DIM
    return pl.pallas_call(
        _sb_kernel,
        grid=(bsz, npairs, nq),
        in_specs=[
            pl.BlockSpec((t, LANES), lambda b, p, i: (b * nq + i, qcol + p)),
            pl.BlockSpec((seq, LANES), lambda b, p, i: (b, kcol + p)),
            pl.BlockSpec((seq, LANES), lambda b, p, i: (b, vcol + p)),
            pl.BlockSpec((t, t + LANES), lambda b, p, i: (0, 0)),
        ],
        out_specs=pl.BlockSpec((t, LANES), lambda b, p, i: (b * nq + i, p)),
        out_shape=jax.ShapeDtypeStruct((bsz * seq, npairs * LANES), BF16),
        scratch_shapes=[pltpu.VMEM((nh, t, LANES), F32), pltpu.VMEM((nh, t, LANES), F32)],
        compiler_params=_params(("arbitrary", "arbitrary", "arbitrary")),
        name="sb_attn",
    )(proj, proj, proj, u_mat)


def _memkv_kernel(m_ref, g_ref, w_ref, o_ref):
    x = m_ref[...]
    n = x * lax.rsqrt(jnp.mean(x * x, axis=-1, keepdims=True) + EPS) * g_ref[...]
    o_ref[...] = _dot(n.astype(BF16), w_ref[...]).astype(BF16)


def mem_kv(mem2, g, w, mlen):
    rows, d = mem2.shape
    n = w.shape[1]
    return pl.pallas_call(
        _memkv_kernel,
        grid=(rows // mlen,),
        in_specs=[pl.BlockSpec((mlen, d), lambda b: (b, 0)),
                  pl.BlockSpec((1, d), lambda b: (0, 0)),
                  pl.BlockSpec((d, n), lambda b: (0, 0))],
        out_specs=pl.BlockSpec((mlen, n), lambda b: (b, 0)),
        out_shape=jax.ShapeDtypeStruct((rows, n), BF16),
        compiler_params=_params(("arbitrary",)),
        name="mem_kv",
    )(mem2, g, w)


def _merge_kernel(x_ref, yssd_ref, ysb_ref, gssd_ref, gsb_ref, wso_ref, wsbo_ref, wo_ref, gx_ref,
                  wxq_ref, kv_ref, wxo_ref, gm_ref, wrh_ref, wrl_ref, br_ref, tri_ref,
                  h2_ref, hn_ref, eidx_ref, rank_ref, gate_ref, cnt_ref, cnt_scr):
    @pl.when(pl.program_id(0) == 0)
    def _():
        cnt_scr[...] = jnp.zeros(cnt_scr.shape, F32)

    tm = x_ref.shape[0]
    y_ssd = _dot(yssd_ref[...], wso_ref[...])
    y_sb = _dot(ysb_ref[...], wsbo_ref[...])
    merged = (_sigmoid(gssd_ref[...].astype(F32)) * y_ssd
              + _sigmoid(gsb_ref[...].astype(F32)) * y_sb)
    h1 = x_ref[...] + _dot(merged.astype(BF16), wo_ref[...])

    n2 = h1 * lax.rsqrt(jnp.mean(h1 * h1, axis=-1, keepdims=True) + EPS) * gx_ref[...]
    q = (_dot(n2.astype(BF16), wxq_ref[...]) * (XA_HEAD_DIM ** -0.5)).astype(BF16)
    xi = XA_HEADS * XA_HEAD_DIM
    heads = []
    for h in range(XA_HEADS):
        kh = kv_ref[:, h * XA_HEAD_DIM:(h + 1) * XA_HEAD_DIM]
        vh = kv_ref[:, xi + h * XA_HEAD_DIM:xi + (h + 1) * XA_HEAD_DIM]
        lg = _dot_nt(q[:, h * XA_HEAD_DIM:(h + 1) * XA_HEAD_DIM], kh)
        ex = jnp.exp(lg - jnp.max(lg, axis=-1, keepdims=True))
        den = jnp.sum(ex, axis=-1, keepdims=True)
        heads.append(_dot(ex.astype(BF16), vh) / den)
    o = jnp.concatenate(heads, axis=1).astype(BF16)
    h2 = h1 + _dot(o, wxo_ref[...])
    h2_ref[...] = h2

    n3 = h2 * lax.rsqrt(jnp.mean(h2 * h2, axis=-1, keepdims=True) + EPS) * gm_ref[...]
    _store_rows(hn_ref, n3)

    nh, nl = _split_bf16(n3)
    lg = _dot(nh, wrh_ref[...]) + _dot(nl, wrh_ref[...]) + _dot(nh, wrl_ref[...]) + br_ref[...]
    lane = lax.broadcasted_iota(I32, (tm, LANES), 1)
    lanef = lane.astype(F32)
    work = lg
    vals, idxs = [], []
    for _ in range(TOP_K):
        m = jnp.max(work, axis=-1, keepdims=True)
        idx = jnp.min(jnp.where(work == m, lanef, float(LANES)), axis=-1, keepdims=True)
        vals.append(m)
        idxs.append(idx)
        work = jnp.where(lanef == idx, NEG_BIG * 2.0, work)
    exps = [jnp.exp(v - vals[0]) for v in vals]
    den = exps[0]
    for ex in exps[1:]:
        den = den + ex

    sel = jnp.zeros((tm, LANES), F32)
    for idx in idxs:
        sel = sel + (lanef == idx).astype(F32)
    incl = _dot(tri_ref[...], sel.astype(BF16))
    before = incl - sel + cnt_scr[...]
    eidx = jnp.zeros((tm, LANES), F32)
    rank = jnp.zeros((tm, LANES), F32)
    gate = jnp.zeros((tm, LANES), F32)
    for k in range(TOP_K):
        rk = jnp.sum(jnp.where(lanef == idxs[k], before, 0.0), axis=-1, keepdims=True)
        eidx = jnp.where(lane == k, idxs[k], eidx)
        rank = jnp.where(lane == k, rk, rank)
        gate = jnp.where(lane == k, exps[k] / den, gate)
    eidx_ref[...] = eidx.astype(I32)
    rank_ref[...] = rank.astype(I32)
    gate_ref[...] = gate
    cnt_scr[...] = cnt_scr[...] + incl[tm - 1:tm, :]
    cnt_ref[...] = cnt_scr[...].astype(I32)


def merge(x2, y_ssd, y_sb, proj, gcol, wso, wsbo, wo, gx, wxq, kv, wxo, gm, wrh, wrl, br, tri,
          seq, mlen, tm):
    t, d = x2.shape
    si = y_ssd.shape[1]
    sbi = y_sb.shape[1]
    xi2 = kv.shape[1]
    full = lambda a: pl.BlockSpec(a.shape, lambda i: (0,) * a.ndim)
    row = lambda w: pl.BlockSpec((tm, w), lambda i: (i, 0))
    return pl.pallas_call(
        _merge_kernel,
        grid=(t // tm,),
        in_specs=[
            row(d), row(si), row(sbi),
            pl.BlockSpec((tm, d), lambda i: (i, gcol)),
            pl.BlockSpec((tm, d), lambda i: (i, gcol + 1)),
            full(wso), full(wsbo), full(wo), full(gx), full(wxq),
            pl.BlockSpec((mlen, xi2), lambda i: ((i * tm) // seq, 0)),
            full(wxo), full(gm), full(wrh), full(wrl), full(br), full(tri),
        ],
        out_specs=[
            row(d),
            pl.BlockSpec((tm * d // LANES, LANES), lambda i: (i, 0)),
            row(LANES), row(LANES), row(LANES),
            pl.BlockSpec((1, LANES), lambda i: (0, 0)),
        ],
        out_shape=[
            jax.ShapeDtypeStruct((t, d), F32),
            jax.ShapeDtypeStruct((t * d // LANES, LANES), F32),
            jax.ShapeDtypeStruct((t, LANES), I32),
            jax.ShapeDtypeStruct((t, LANES), I32),
            jax.ShapeDtypeStruct((t, LANES), F32),
            jax.ShapeDtypeStruct((1, LANES), I32),
        ],
        scratch_shapes=[pltpu.VMEM((1, LANES), F32)],
        compiler_params=_params(("arbitrary",)),
        name="merge",
    )(x2, y_ssd, y_sb, proj, proj, wso, wsbo, wo, gx, wxq, kv, wxo, gm, wrh, wrl, br, tri)


def _index_copies(e_hbm, r_hbm, e_sm, r_sm, isem, step, slot):
    n = e_sm.shape[0] // 2
    return (pltpu.make_async_copy(e_hbm.at[pl.ds(step * n, n)], e_sm.at[pl.ds(slot * n, n)], isem.at[0, slot]),
            pltpu.make_async_copy(r_hbm.at[pl.ds(step * n, n)], r_sm.at[pl.ds(slot * n, n)], isem.at[1, slot]))


def _issue_rows(ps_ref, e_sm, r_sm, slot, tm, copy_for):
    n = tm * TOP_K

    def body(it, c):
        toks = [it * ROW_DMA_UNROLL + u for u in range(ROW_DMA_UNROLL)]
        rows = []
        for tok in toks:
            for k in range(TOP_K):
                a = slot * n + tok * TOP_K + k
                rows.append(ps_ref[e_sm[a]] + r_sm[a])
        for u, tok in enumerate(toks):
            for k in range(TOP_K):
                copy_for(tok, k, rows[u * TOP_K + k]).start(priority=k % 2)
        return c

    lax.fori_loop(0, tm // ROW_DMA_UNROLL, body, 0)


def _dispatch_kernel(ps_ref, lo_ref, hi_ref, e_hbm, r_hbm, hn_ref, xin_out, e_sm, r_sm, zero_scr,
                     isem, sem, zsem):
    n = e_sm.shape[0] // 2
    tm = n // TOP_K
    i = pl.program_id(0)
    last = pl.num_programs(0) - 1
    slot = i % 2
    idx = functools.partial(_index_copies, e_hbm, r_hbm, e_sm, r_sm, isem)

    @pl.when(i == 0)
    def _():
        for c in idx(0, 0):
            c.start()

    for c in idx(i, slot):
        c.wait()

    @pl.when(i < last)
    def _():
        for c in idx(i + 1, 1 - slot):
            c.start()

    _issue_rows(ps_ref, e_sm, r_sm, slot, tm,
                lambda tok, k, row: pltpu.make_async_copy(_row(hn_ref, tok), _row(xin_out, row), sem))

    @pl.when(i == last)
    def _():
        zero_scr[...] = jnp.zeros(zero_scr.shape, F32)
        zrow = _row(zero_scr, 0)

        def per_span(s, c):
            def zissue(r, c2):
                pltpu.make_async_copy(zrow, _row(xin_out, r), zsem).start()
                return c2

            def zdrain(r, c2):
                pltpu.make_async_copy(zrow, _row(xin_out, r), zsem).wait()
                return c2

            lax.fori_loop(lo_ref[s], hi_ref[s], zissue, 0)
            lax.fori_loop(lo_ref[s], hi_ref[s], zdrain, 0)
            return c

        lax.fori_loop(0, lo_ref.shape[0], per_span, 0)

    for k in range(TOP_K):
        pltpu.make_async_copy(hn_ref, xin_out.at[pl.ds(0, tm * SUBLANES), :], sem).wait()


def dispatch(pstart, pad_lo, pad_hi, e_flat, r_flat, hn, n_rows, tm=512):
    t = hn.shape[0] // SUBLANES
    grid_spec = pltpu.PrefetchScalarGridSpec(
        num_scalar_prefetch=3,
        grid=(t // tm,),
        in_specs=[pl.BlockSpec(memory_space=pl.ANY), pl.BlockSpec(memory_space=pl.ANY),
                  pl.BlockSpec((tm * SUBLANES, LANES), lambda i, ps, lo, hi: (i, 0))],
        out_specs=pl.BlockSpec(memory_space=pl.ANY),
        scratch_shapes=[pltpu.SMEM((2 * tm * TOP_K,), I32), pltpu.SMEM((2 * tm * TOP_K,), I32),
                        pltpu.VMEM((SUBLANES, LANES), F32),
                        pltpu.SemaphoreType.DMA((2, 2)), pltpu.SemaphoreType.DMA,
                        pltpu.SemaphoreType.DMA],
    )
    return pl.pallas_call(
        _dispatch_kernel,
        grid_spec=grid_spec,
        out_shape=jax.ShapeDtypeStruct((n_rows * SUBLANES, LANES), F32),
        compiler_params=_params(("arbitrary",)),
        name="dispatch",
    )(pstart, pad_lo, pad_hi, e_flat, r_flat, hn)


def _moe_kernel(be_ref, nu_ref, x_ref, wgu_ref, bgu_ref, wd_ref, bd_ref, o_ref, wgu_scr, wd_scr):
    i = pl.program_id(0)
    dff = wd_ref.shape[1]
    active = i < nu_ref[0]

    @pl.when(active & ((i == 0) | (be_ref[i] != be_ref[jnp.maximum(i - 1, 0)])))
    def _():
        for r in range(0, wgu_ref.shape[1], MOE_CAST_ROWS):
            wgu_scr[r:r + MOE_CAST_ROWS, :] = wgu_ref[0, r:r + MOE_CAST_ROWS, :].astype(BF16)
        for r in range(0, dff, MOE_CAST_ROWS):
            wd_scr[r:r + MOE_CAST_ROWS, :] = wd_ref[0, r:r + MOE_CAST_ROWS, :].astype(BF16)

    @pl.when(active)
    def _():
        x = _load_rows(x_ref, MOE_BLOCK).astype(BF16)
        gu = _dot(x, wgu_scr[...]) + bgu_ref[0]
        g_lin = jnp.minimum(gu[:, :dff], SWIGLU_LIMIT)
        u_lin = jnp.clip(gu[:, dff:], -SWIGLU_LIMIT, SWIGLU_LIMIT)
        act = g_lin * _sigmoid(SWIGLU_ALPHA * g_lin) * (u_lin + 1.0)
        _store_rows(o_ref, _dot(act.astype(BF16), wd_scr[...]) + bd_ref[0])

    @pl.when(jnp.logical_not(active))
    def _():
        o_ref[...] = jnp.zeros(o_ref.shape, F32)


def moe_ffn(blk_expert, n_used, xin, wgu, bgu, wd, bd):
    blk_rows = MOE_BLOCK * SUBLANES
    nb = xin.shape[0] // blk_rows
    d = wgu.shape[1]
    f2 = wgu.shape[2]
    dff = wd.shape[1]
    grid_spec = pltpu.PrefetchScalarGridSpec(
        num_scalar_prefetch=2,
        grid=(nb,),
        in_specs=[
            pl.BlockSpec((blk_rows, LANES), lambda i, be, nu: (jnp.minimum(i, nu[0] - 1), 0)),
            pl.BlockSpec((1, d, f2), lambda i, be, nu: (be[i], 0, 0)),
            pl.BlockSpec((1, 1, f2), lambda i, be, nu: (be[i], 0, 0)),
            pl.BlockSpec((1, dff, d), lambda i, be, nu: (be[i], 0, 0)),
            pl.BlockSpec((1, 1, d), lambda i, be, nu: (be[i], 0, 0)),
        ],
        out_specs=pl.BlockSpec((blk_rows, LANES), lambda i, be, nu: (i, 0)),
        scratch_shapes=[pltpu.VMEM((d, f2), BF16), pltpu.VMEM((dff, d), BF16)],
    )
    return pl.pallas_call(
        _moe_kernel,
        grid_spec=grid_spec,
        out_shape=jax.ShapeDtypeStruct(xin.shape, F32),
        compiler_params=_params(("arbitrary",)),
        name="moe_ffn",
    )(blk_expert, n_used, xin, wgu, bgu, wd, bd)


def _combine_kernel(ps_ref, e_hbm, r_hbm, yb_hbm, h2_ref, gate_ref, gf_ref, o_ref,
                    e_sm, r_sm, buf, isem, sem):
    tm = h2_ref.shape[0]
    i = pl.program_id(0)
    last = pl.num_programs(0) - 1
    slot = i % 2
    idx = functools.partial(_index_copies, e_hbm, r_hbm, e_sm, r_sm, isem)

    def gather(s):
        _issue_rows(ps_ref, e_sm, r_sm, s, tm,
                    lambda tok, k, row: pltpu.make_async_copy(_row(yb_hbm, row), _row(buf.at[s, k], tok),
                                                              sem.at[s]))

    @pl.when(i == 0)
    def _():
        for c in idx(0, 0):
            c.start()
        for c in idx(0, 0):
            c.wait()
        gather(0)

        @pl.when(last > 0)
        def _():
            for c in idx(1, 1):
                c.start()

    @pl.when(i < last)
    def _():
        for c in idx(i + 1, 1 - slot):
            c.wait()
        gather(1 - slot)

    @pl.when(i + 2 <= last)
    def _():
        for c in idx(i + 2, slot):
            c.start()

    for k in range(TOP_K):
        pltpu.make_async_copy(yb_hbm.at[pl.ds(0, tm * SUBLANES), :], buf.at[slot, k], sem.at[slot]).wait()

    gate = gate_ref[...]
    h3 = h2_ref[...]
    for k in range(TOP_K):
        h3 = h3 + gate[:, k:k + 1] * _load_rows(buf.at[slot, k], tm)
    o_ref[...] = h3 * lax.rsqrt(jnp.mean(h3 * h3, axis=-1, keepdims=True) + EPS) * gf_ref[...]


def combine(pstart, e_flat, r_flat, yb, h2, gate, gf, tm=256):
    t, d = h2.shape
    grid_spec = pltpu.PrefetchScalarGridSpec(
        num_scalar_prefetch=1,
        grid=(t // tm,),
        in_specs=[
            pl.BlockSpec(memory_space=pl.ANY),
            pl.BlockSpec(memory_space=pl.ANY),
            pl.BlockSpec(memory_space=pl.ANY),
            pl.BlockSpec((tm, d), lambda i, ps: (i, 0)),
            pl.BlockSpec((tm, LANES), lambda i, ps: (i, 0)),
            pl.BlockSpec((1, d), lambda i, ps: (0, 0)),
        ],
        out_specs=pl.BlockSpec((tm, d), lambda i, ps: (i, 0)),
        scratch_shapes=[pltpu.SMEM((2 * tm * TOP_K,), I32), pltpu.SMEM((2 * tm * TOP_K,), I32),
                        pltpu.VMEM((2, TOP_K, tm * SUBLANES, LANES), F32),
                        pltpu.SemaphoreType.DMA((2, 2)), pltpu.SemaphoreType.DMA((2,))],
    )
    return pl.pallas_call(
        _combine_kernel,
        grid_spec=grid_spec,
        out_shape=jax.ShapeDtypeStruct((t, d), F32),
        compiler_params=_params(("arbitrary",)),
        name="combine",
    )(pstart, e_flat, r_flat, yb, h2, gate, gf)


def _pad_lanes(v, fill=0.0):
    return jnp.pad(v.reshape(1, -1).astype(F32), ((0, 0), (0, LANES - v.shape[-1])), constant_values=fill)


def _layer(h, mem, p, merge_tm=512):
    bsz, seq, d = h.shape
    assert d == SUBLANES * LANES, "the row-tile layout needs one (8,128) tile per model row"
    t = bsz * seq
    mlen = mem.shape[1]
    inner = SSD_HEADS * SSD_HEAD_DIM
    conv_ch = inner + 2 * SSD_GROUPS * SSD_STATE
    sbi = SB_HEADS * SB_HEAD_DIM
    n_experts = p["w_router"].shape[1]
    x2 = h.reshape(t, d)

    w_in = p["w_in"]
    dt0 = inner + conv_ch
    w_main = jnp.concatenate([w_in[:, :dt0], w_in[:, dt0 + SSD_HEADS:]], axis=1).astype(BF16)
    w_dt = jnp.pad(w_in[:, dt0:dt0 + SSD_HEADS], ((0, 0), (0, LANES - SSD_HEADS))).astype(BF16)
    proj, dt_raw = inproj(x2, p["g_mix"].reshape(1, d), w_main, w_dt)
    qcol = dt0 // LANES
    kcol = qcol + sbi // LANES
    vcol = kcol + sbi // LANES
    gcol = (dt0 + 3 * sbi) // d

    heads = jnp.arange(LANES)[:, None]
    chan = jnp.arange(inner)[None, :]
    e_mat = (chan // SSD_HEAD_DIM == heads).astype(BF16)
    y_ssd = ssd(proj, dt_raw, p["conv_w"], p["conv_b"].reshape(1, conv_ch), _pad_lanes(p["dt_bias"]),
                _pad_lanes(p["a_log"]), jnp.repeat(p["d_skip"], SSD_HEAD_DIM).reshape(1, inner),
                p["g_ssd"].reshape(1, inner), e_mat, bsz, seq)

    jj = jnp.arange(SB_T)[:, None]
    ss = jnp.arange(SB_T + LANES)[None, :]
    u_mat = ((ss >= SB_T) | (jj > ss)).astype(BF16)
    y_sb = sb_attn(proj, u_mat, bsz, seq, qcol, kcol, vcol)

    kv = mem_kv(mem.reshape(bsz * mlen, d), p["g_mem"].reshape(1, d), p["w_xkv"].astype(BF16), mlen)

    wr = jnp.pad(p["w_router"], ((0, 0), (0, LANES - n_experts)))
    wrh = wr.astype(BF16)
    wrl = (wr - wrh.astype(F32)).astype(BF16)
    tri = (jnp.arange(merge_tm)[None, :] <= jnp.arange(merge_tm)[:, None]).astype(BF16)
    h2, hn, eidx, rank, gate, counts = merge(
        x2, y_ssd, y_sb, proj, gcol, p["w_ssd_out"].astype(BF16), p["w_sb_out"].astype(BF16),
        p["w_out"].astype(BF16), p["g_xattn"].reshape(1, d), p["w_xq"].astype(BF16), kv,
        p["w_xo"].astype(BF16), p["g_moe"].reshape(1, d), wrh, wrl,
        _pad_lanes(p["b_router"], NEG_BIG), tri, seq, mlen, merge_tm)

    cnt = counts[0, :n_experts]
    padded = (cnt + MOE_BLOCK - 1) // MOE_BLOCK * MOE_BLOCK
    pend = jnp.cumsum(padded)
    pstart = (pend - padded).astype(I32)
    n_blocks = -(-(t * TOP_K) // MOE_BLOCK) + n_experts
    n_rows = n_blocks * MOE_BLOCK
    n_used = (pend[-1] // MOE_BLOCK).astype(I32)
    blk = jnp.minimum(jnp.arange(n_blocks, dtype=I32), n_used - 1) * MOE_BLOCK
    blk_expert = jnp.sum(blk[:, None] >= pend[None, :n_experts - 1], axis=1).astype(I32)
    e_flat = eidx[:, :TOP_K].reshape(-1)
    r_flat = rank[:, :TOP_K].reshape(-1)
    pad_lo = jnp.concatenate([pstart + cnt, pend[-1:]]).astype(I32)
    pad_hi = jnp.concatenate([pend, jnp.full((1,), n_rows)]).astype(I32)

    xin = dispatch(pstart, pad_lo, pad_hi, e_flat, r_flat, hn, n_rows)
    yb = moe_ffn(blk_expert, n_used.reshape(1), xin, p["w_gu"], p["b_gu"][:, None, :],
                 p["w_down"], p["b_down"][:, None, :])
    return h2, (pstart, e_flat, r_flat, yb, gate)


def kernel(x, mem, g_mix, w_in, conv_w, conv_b, dt_bias, a_log, d_skip, g_ssd, w_ssd_out, w_sb_out,
           w_out, g_xattn, g_mem, w_xq, w_xkv, w_xo, g_moe, w_router, b_router, w_gu, b_gu, w_down,
           b_down, g_final):
    stacked = dict(g_mix=g_mix, w_in=w_in, conv_w=conv_w, conv_b=conv_b, dt_bias=dt_bias, a_log=a_log,
                   d_skip=d_skip, g_ssd=g_ssd, w_ssd_out=w_ssd_out, w_sb_out=w_sb_out, w_out=w_out,
                   g_xattn=g_xattn, g_mem=g_mem, w_xq=w_xq, w_xkv=w_xkv, w_xo=w_xo, g_moe=g_moe,
                   w_router=w_router, b_router=b_router, w_gu=w_gu, b_gu=b_gu, w_down=w_down,
                   b_down=b_down)
    assert g_mix.shape[0] == 1, "stacked depth other than 1 is not supported"
    bsz, seq, d = x.shape
    p = {k: v[0] for k, v in stacked.items()}
    h2, (pstart, e_flat, r_flat, yb, gate) = _layer(x, mem, p)
    out = combine(pstart, e_flat, r_flat, yb, h2, gate, g_final.reshape(1, d))
    return out.reshape(bsz, seq, d)
```

```python
import functools

import jax
import jax.numpy as jnp
from jax import lax
from jax.experimental import pallas as pl
from jax.experimental.pallas import tpu as pltpu

F32 = jnp.float32
BF16 = jnp.bfloat16
I32 = jnp.int32

LANES = 128
SUBLANES = 8
EPS = 1e-6
VMEM_LIMIT = 56 * 1024 * 1024

SSD_HEADS = 16
SSD_HEAD_DIM = 64
SSD_GROUPS = 2
SSD_STATE = 128
SSD_CONV = 4
SB_HEADS = 8
SB_HEAD_DIM = 64
XA_HEADS = 4
XA_HEAD_DIM = 128
TOP_K = 4
SWIGLU_LIMIT = 7.0
SWIGLU_ALPHA = 1.702

SSD_L = 128
SB_T = 256
SB_SKIP = 110.0
MOE_BLOCK = 512
MOE_CAST_ROWS = 128
ROW_DMA_UNROLL = 2
NEG_BIG = -1e30


def _dot(a, b):
    return jnp.dot(a, b, preferred_element_type=F32)


def _dot_nt(a, b):
    return lax.dot_general(a, b, (((1,), (1,)), ((), ())), preferred_element_type=F32)


def _split_bf16(v):
    hi = v.astype(BF16)
    lo = (v - hi.astype(F32)).astype(BF16)
    return hi, lo


def _dot2(v, m):
    hi, lo = _split_bf16(v)
    return _dot(hi, m) + _dot(lo, m)


def _softplus(x):
    return jnp.maximum(x, 0.0) + jnp.log(1.0 + jnp.exp(-jnp.abs(x)))


def _sigmoid(x):
    return 1.0 / (1.0 + jnp.exp(-x))


def _params(sem, vmem=VMEM_LIMIT):
    return pltpu.CompilerParams(dimension_semantics=sem, vmem_limit_bytes=vmem)


def _store_rows(ref, val):
    n = val.shape[0]
    for s in range(SUBLANES):
        ref[pl.ds(s, n, stride=SUBLANES), :] = val[:, s * LANES:(s + 1) * LANES]


def _load_rows(ref, n):
    return jnp.concatenate([ref[pl.ds(s, n, stride=SUBLANES), :] for s in range(SUBLANES)], axis=1)


def _row(ref, r):
    return ref.at[pl.ds(pl.multiple_of(r * SUBLANES, SUBLANES), SUBLANES), :]


def _inproj_kernel(x_ref, g_ref, w_ref, wdt_ref, proj_ref, dt_ref, n_scr):
    @pl.when(pl.program_id(1) == 0)
    def _():
        x = x_ref[...]
        n = x * lax.rsqrt(jnp.mean(x * x, axis=-1, keepdims=True) + EPS) * g_ref[...]
        nb = n.astype(BF16)
        n_scr[...] = nb
        dt_ref[...] = _dot(nb, wdt_ref[...])

    proj_ref[...] = _dot(n_scr[...], w_ref[...]).astype(BF16)


def inproj(x2, g, w, wdt, tm=1024, tn=1024):
    t, d = x2.shape
    n = w.shape[1]
    return pl.pallas_call(
        _inproj_kernel,
        grid=(t // tm, n // tn),
        in_specs=[
            pl.BlockSpec((tm, d), lambda i, j: (i, 0)),
            pl.BlockSpec((1, d), lambda i, j: (0, 0)),
            pl.BlockSpec((d, tn), lambda i, j: (0, j)),
            pl.BlockSpec((d, LANES), lambda i, j: (0, 0)),
        ],
        out_specs=[
            pl.BlockSpec((tm, tn), lambda i, j: (i, j)),
            pl.BlockSpec((tm, LANES), lambda i, j: (i, 0)),
        ],
        out_shape=[
            jax.ShapeDtypeStruct((t, n), BF16),
            jax.ShapeDtypeStruct((t, LANES), F32),
        ],
        scratch_shapes=[pltpu.VMEM((tm, d), BF16)],
        compiler_params=_params(("arbitrary", "arbitrary")),
        name="inproj",
    )(x2, g, w, wdt)


def _ssd_kernel(z_ref, xs_ref, bc_ref, dt_ref, cw_ref, cb_ref, dtb_ref, alog_ref, dskip_ref,
                gn_ref, e_ref, o_ref, ext_scr, state_scr):
    L = SSD_L
    inner = SSD_HEADS * SSD_HEAD_DIM
    gw = inner // SSD_GROUPS
    n = SSD_STATE

    @pl.when(pl.program_id(1) == 0)
    def _():
        ext_scr[0:SUBLANES, :] = jnp.zeros((SUBLANES, ext_scr.shape[1]), F32)
        state_scr[...] = jnp.zeros(state_scr.shape, F32)

    ext_scr[SUBLANES:SUBLANES + L, 0:inner] = xs_ref[...].astype(F32)
    ext_scr[SUBLANES:SUBLANES + L, inner:] = bc_ref[...].astype(F32)
    conv = cb_ref[...]
    for k in range(SSD_CONV):
        off = SUBLANES - (SSD_CONV - 1) + k
        conv = conv + cw_ref[k:k + 1, :] * ext_scr[off:off + L, :]
    ext_scr[0:SUBLANES, :] = ext_scr[L:L + SUBLANES, :]
    xbc = conv * _sigmoid(conv)
    xs = xbc[:, :inner]

    dt = _softplus(dt_ref[...] + dtb_ref[...])
    da = dt * (-jnp.exp(alog_ref[...]))
    row = lax.broadcasted_iota(I32, (L, L), 0)
    col = lax.broadcasted_iota(I32, (L, L), 1)
    tril = col <= row
    cs = jnp.dot(tril.astype(F32), da, precision=lax.Precision.HIGHEST,
                 preferred_element_type=F32)
    cst = cs.T
    cs_last = cs[L - 1:L, :]

    e = e_ref[...]
    xdt = xs * _dot2(dt, e)
    xw = (xdt * _dot2(jnp.exp(cs_last - cs), e)).astype(BF16)
    od_e = _dot2(jnp.exp(cs), e)
    xdt_b = xdt.astype(BF16)
    lane = lax.broadcasted_iota(I32, (L, LANES), 1)

    y_parts = []
    new_states = []
    for g in range(SSD_GROUPS):
        bg = xbc[:, inner + g * n:inner + (g + 1) * n]
        cg = xbc[:, inner + SSD_GROUPS * n + g * n:inner + SSD_GROUPS * n + (g + 1) * n]
        bg_b = bg.astype(BF16)
        cg_b = cg.astype(BF16)
        cbm = _dot_nt(cg_b, bg_b)
        st = state_scr[:, g * gw:(g + 1) * gw]
        y_off = _dot(cg_b, st.astype(BF16))
        new_states.append(_dot(bg.T.astype(BF16), xw[:, g * gw:(g + 1) * gw]))
        heads_per_group = SSD_HEADS // SSD_GROUPS
        for hp in range(heads_per_group // 2):
            pair = xdt_b[:, g * gw + hp * LANES:g * gw + (hp + 1) * LANES]
            yd = []
            for sub in range(2):
                h = g * heads_per_group + hp * 2 + sub
                seg = cs[:, h:h + 1] - cst[h:h + 1, :]
                lm = jnp.exp(jnp.where(tril, seg, -jnp.inf))
                yd.append(_dot((cbm * lm).astype(BF16), pair))
            y_parts.append((jnp.where(lane < SSD_HEAD_DIM, yd[0], yd[1]), g, hp, y_off))

    cols = []
    for (yd, g, hp, y_off) in y_parts:
        c0 = g * gw + hp * LANES
        cols.append(yd + y_off[:, hp * LANES:(hp + 1) * LANES] * od_e[:, c0:c0 + LANES])
    y = jnp.concatenate(cols, axis=1) + xs * dskip_ref[...]
    zf = z_ref[...].astype(F32)
    y = y * (zf * _sigmoid(zf))
    outs = []
    for g in range(SSD_GROUPS):
        yg = y[:, g * gw:(g + 1) * gw]
        ms = jnp.mean(yg * yg, axis=-1, keepdims=True)
        outs.append(yg * lax.rsqrt(ms + EPS) * gn_ref[:, g * gw:(g + 1) * gw])
    o_ref[...] = jnp.concatenate(outs, axis=1).astype(BF16)

    cd_e = od_e[L - 1:L, :]
    for g in range(SSD_GROUPS):
        state_scr[:, g * gw:(g + 1) * gw] = (
            state_scr[:, g * gw:(g + 1) * gw] * cd_e[:, g * gw:(g + 1) * gw] + new_states[g])


def ssd(proj, dt_raw, conv_w, conv_b, dtb, alog, dskip_e, gn, e_mat, bsz, seq):
    L = SSD_L
    inner = SSD_HEADS * SSD_HEAD_DIM
    bcw = 2 * SSD_GROUPS * SSD_STATE
    nc = seq // L
    ch = inner + bcw
    full = lambda shape: pl.BlockSpec(shape, lambda b, c: (0,) * len(shape))
    return pl.pallas_call(
        _ssd_kernel,
        grid=(bsz, nc),
        in_specs=[
            pl.BlockSpec((L, inner), lambda b, c: (b * nc + c, 0)),
            pl.BlockSpec((L, inner), lambda b, c: (b * nc + c, 1)),
            pl.BlockSpec((L, bcw), lambda b, c: (b * nc + c, 2 * inner // bcw)),
            pl.BlockSpec((L, LANES), lambda b, c: (b * nc + c, 0)),
            full((SSD_CONV, ch)), full((1, ch)), full((1, LANES)), full((1, LANES)),
            full((1, inner)), full((1, inner)), full((LANES, inner)),
        ],
        out_specs=pl.BlockSpec((L, inner), lambda b, c: (b * nc + c, 0)),
        out_shape=jax.ShapeDtypeStruct((bsz * seq, inner), BF16),
        scratch_shapes=[pltpu.VMEM((L + 2 * SUBLANES, ch), F32), pltpu.VMEM((SSD_STATE, inner), F32)],
        compiler_params=_params(("arbitrary", "arbitrary")),
        name="ssd",
    )(proj, proj, proj, dt_raw, conv_w, conv_b, dtb, alog, dskip_e, gn, e_mat)


def _sb_kernel(q_ref, k_ref, v_ref, u_ref, o_ref, carry_scr, acc_scr):
    t = SB_T
    nh = LANES // SB_HEAD_DIM
    i = pl.program_id(2)
    lane = lax.broadcasted_iota(I32, (t, LANES), 1)
    q = q_ref[...]
    scale = SB_HEAD_DIM ** -0.5
    qh = []
    for h in range(nh):
        in_head = (lane >= h * SB_HEAD_DIM) & (lane < (h + 1) * SB_HEAD_DIM)
        qh.append((jnp.where(in_head, q, jnp.zeros_like(q)).astype(F32) * scale).astype(BF16))
    u = u_ref[...]

    def step(j, diagonal, state):
        ks = pl.multiple_of(j * t, t)
        kb = k_ref[pl.ds(ks, t), :]
        vb = v_ref[pl.ds(ks, t), :]
        if diagonal:
            mask = lax.broadcasted_iota(I32, (t, t), 1) < lax.broadcasted_iota(I32, (t, t), 0)
        out = []
        for h in range(nh):
            x = _dot_nt(qh[h], kb)
            sp = _softplus(x)
            if diagonal:
                sp = jnp.where(mask, sp, 0.0)
            suf = _dot(sp.astype(BF16), u)
            e = x - sp - suf[:, :t]
            if state is not None:
                e = e - jnp.concatenate([state[h][0]] * (t // LANES), axis=1)
            w = jnp.exp(e)
            if diagonal:
                w = jnp.where(mask, w, 0.0)
            pv = _dot(w.astype(BF16), vb)
            if state is None:
                out.append((suf[:, t:], pv))
            else:
                out.append((state[h][0] + suf[:, t:], state[h][1] + pv))
        return out

    def smallest(state):
        return functools.reduce(jnp.minimum, [jnp.min(c) for c, _ in state])

    def save(state):
        for h in range(nh):
            carry_scr[h] = state[h][0]
            acc_scr[h] = state[h][1]

    def cond(st):
        j, go = st
        return (j >= 0) & (go > 0)

    def body(st):
        j, _ = st
        state = step(j, False, [(carry_scr[h], acc_scr[h]) for h in range(nh)])
        save(state)
        return j - 1, (smallest(state) < SB_SKIP).astype(I32)

    @pl.when(i == 0)
    def _():
        save(step(0, True, None))

    @pl.when(i > 0)
    def _():
        state = step(i - 1, False, step(i, True, None))
        save(state)
        lax.while_loop(cond, body, (i - 2, (smallest(state) < SB_SKIP).astype(I32)))

    o_ref[...] = jnp.where(lane < SB_HEAD_DIM, acc_scr[0], acc_scr[1]).astype(BF16)


def sb_attn(proj, u_mat, bsz, seq, qcol, kcol, vcol):
    t = SB_T
    nq = seq // t
    npairs = SB_HEADS * SB_HEAD_DIM // LANES
    nh = LANES // SB_HEAD_DIM
    return pl.pallas_call(
        _sb_kernel,
        grid=(bsz, npairs, nq),
        in_specs=[
            pl.BlockSpec((t, LANES), lambda b, p, i: (b * nq + i, qcol + p)),
            pl.BlockSpec((seq, LANES), lambda b, p, i: (b, kcol + p)),
            pl.BlockSpec((seq, LANES), lambda b, p, i: (b, vcol + p)),
            pl.BlockSpec((t, t + LANES), lambda b, p, i: (0, 0)),
        ],
        out_specs=pl.BlockSpec((t, LANES), lambda b, p, i: (b * nq + i, p)),
        out_shape=jax.ShapeDtypeStruct((bsz * seq, npairs * LANES), BF16),
        scratch_shapes=[pltpu.VMEM((nh, t, LANES), F32), pltpu.VMEM((nh, t, LANES), F32)],
        compiler_params=_params(("arbitrary", "arbitrary", "arbitrary")),
        name="sb_attn",
    )(proj, proj, proj, u_mat)


def _memkv_kernel(m_ref, g_ref, w_ref, o_ref):
    x = m_ref[...]
    n = x * lax.rsqrt(jnp.mean(x * x, axis=-1, keepdims=True) + EPS) * g_ref[...]
    o_ref[...] = _dot(n.astype(BF16), w_ref[...]).astype(BF16)


def mem_kv(mem2, g, w, mlen):
    rows, d = mem2.shape
    n = w.shape[1]
    return pl.pallas_call(
        _memkv_kernel,
        grid=(rows // mlen,),
        in_specs=[pl.BlockSpec((mlen, d), lambda b: (b, 0)),
                  pl.BlockSpec((1, d), lambda b: (0, 0)),
                  pl.BlockSpec((d, n), lambda b: (0, 0))],
        out_specs=pl.BlockSpec((mlen, n), lambda b: (b, 0)),
        out_shape=jax.ShapeDtypeStruct((rows, n), BF16),
        compiler_params=_params(("arbitrary",)),
        name="mem_kv",
    )(mem2, g, w)


def _merge_kernel(x_ref, yssd_ref, ysb_ref, gssd_ref, gsb_ref, wso_ref, wsbo_ref, wo_ref, gx_ref,
                  wxq_ref, kv_ref, wxo_ref, gm_ref, wrh_ref, wrl_ref, br_ref, tri_ref,
                  h2_ref, hn_ref, eidx_ref, rank_ref, gate_ref, cnt_ref, cnt_scr):
    @pl.when(pl.program_id(0) == 0)
    def _():
        cnt_scr[...] = jnp.zeros(cnt_scr.shape, F32)

    tm = x_ref.shape[0]
    y_ssd = _dot(yssd_ref[...], wso_ref[...])
    y_sb = _dot(ysb_ref[...], wsbo_ref[...])
    merged = (_sigmoid(gssd_ref[...].astype(F32)) * y_ssd
              + _sigmoid(gsb_ref[...].astype(F32)) * y_sb)
    h1 = x_ref[...] + _dot(merged.astype(BF16), wo_ref[...])

    n2 = h1 * lax.rsqrt(jnp.mean(h1 * h1, axis=-1, keepdims=True) + EPS) * gx_ref[...]
    q = (_dot(n2.astype(BF16), wxq_ref[...]) * (XA_HEAD_DIM ** -0.5)).astype(BF16)
    xi = XA_HEADS * XA_HEAD_DIM
    heads = []
    for h in range(XA_HEADS):
        kh = kv_ref[:, h * XA_HEAD_DIM:(h + 1) * XA_HEAD_DIM]
        vh = kv_ref[:, xi + h * XA_HEAD_DIM:xi + (h + 1) * XA_HEAD_DIM]
        lg = _dot_nt(q[:, h * XA_HEAD_DIM:(h + 1) * XA_HEAD_DIM], kh)
        ex = jnp.exp(lg - jnp.max(lg, axis=-1, keepdims=True))
        den = jnp.sum(ex, axis=-1, keepdims=True)
        heads.append(_dot(ex.astype(BF16), vh) / den)
    o = jnp.concatenate(heads, axis=1).astype(BF16)
    h2 = h1 + _dot(o, wxo_ref[...])
    h2_ref[...] = h2

    n3 = h2 * lax.rsqrt(jnp.mean(h2 * h2, axis=-1, keepdims=True) + EPS) * gm_ref[...]
    _store_rows(hn_ref, n3)

    nh, nl = _split_bf16(n3)
    lg = _dot(nh, wrh_ref[...]) + _dot(nl, wrh_ref[...]) + _dot(nh, wrl_ref[...]) + br_ref[...]
    lane = lax.broadcasted_iota(I32, (tm, LANES), 1)
    lanef = lane.astype(F32)
    work = lg
    vals, idxs = [], []
    for _ in range(TOP_K):
        m = jnp.max(work, axis=-1, keepdims=True)
        idx = jnp.min(jnp.where(work == m, lanef, float(LANES)), axis=-1, keepdims=True)
        vals.append(m)
        idxs.append(idx)
        work = jnp.where(lanef == idx, NEG_BIG * 2.0, work)
    exps = [jnp.exp(v - vals[0]) for v in vals]
    den = exps[0]
    for ex in exps[1:]:
        den = den + ex

    sel = jnp.zeros((tm, LANES), F32)
    for idx in idxs:
        sel = sel + (lanef == idx).astype(F32)
    incl = _dot(tri_ref[...], sel.astype(BF16))
    before = incl - sel + cnt_scr[...]
    eidx = jnp.zeros((tm, LANES), F32)
    rank = jnp.zeros((tm, LANES), F32)
    gate = jnp.zeros((tm, LANES), F32)
    for k in range(TOP_K):
        rk = jnp.sum(jnp.where(lanef == idxs[k], before, 0.0), axis=-1, keepdims=True)
        eidx = jnp.where(lane == k, idxs[k], eidx)
        rank = jnp.where(lane == k, rk, rank)
        gate = jnp.where(lane == k, exps[k] / den, gate)
    eidx_ref[...] = eidx.astype(I32)
    rank_ref[...] = rank.astype(I32)
    gate_ref[...] = gate
    cnt_scr[...] = cnt_scr[...] + incl[tm - 1:tm, :]
    cnt_ref[...] = cnt_scr[...].astype(I32)


def merge(x2, y_ssd, y_sb, proj, gcol, wso, wsbo, wo, gx, wxq, kv, wxo, gm, wrh, wrl, br, tri,
          seq, mlen, tm):
    t, d = x2.shape
    si = y_ssd.shape[1]
    sbi = y_sb.shape[1]
    xi2 = kv.shape[1]
    full = lambda a: pl.BlockSpec(a.shape, lambda i: (0,) * a.ndim)
    row = lambda w: pl.BlockSpec((tm, w), lambda i: (i, 0))
    return pl.pallas_call(
        _merge_kernel,
        grid=(t // tm,),
        in_specs=[
            row(d), row(si), row(sbi),
            pl.BlockSpec((tm, d), lambda i: (i, gcol)),
            pl.BlockSpec((tm, d), lambda i: (i, gcol + 1)),
            full(wso), full(wsbo), full(wo), full(gx), full(wxq),
            pl.BlockSpec((mlen, xi2), lambda i: ((i * tm) // seq, 0)),
            full(wxo), full(gm), full(wrh), full(wrl), full(br), full(tri),
        ],
        out_specs=[
            row(d),
            pl.BlockSpec((tm * d // LANES, LANES), lambda i: (i, 0)),
            row(LANES), row(LANES), row(LANES),
            pl.BlockSpec((1, LANES), lambda i: (0, 0)),
        ],
        out_shape=[
            jax.ShapeDtypeStruct((t, d), F32),
            jax.ShapeDtypeStruct((t * d // LANES, LANES), F32),
            jax.ShapeDtypeStruct((t, LANES), I32),
            jax.ShapeDtypeStruct((t, LANES), I32),
            jax.ShapeDtypeStruct((t, LANES), F32),
            jax.ShapeDtypeStruct((1, LANES), I32),
        ],
        scratch_shapes=[pltpu.VMEM((1, LANES), F32)],
        compiler_params=_params(("arbitrary",)),
        name="merge",
    )(x2, y_ssd, y_sb, proj, proj, wso, wsbo, wo, gx, wxq, kv, wxo, gm, wrh, wrl, br, tri)


def _index_copies(e_hbm, r_hbm, e_sm, r_sm, isem, step, slot):
    n = e_sm.shape[0] // 2
    return (pltpu.make_async_copy(e_hbm.at[pl.ds(step * n, n)], e_sm.at[pl.ds(slot * n, n)], isem.at[0, slot]),
            pltpu.make_async_copy(r_hbm.at[pl.ds(step * n, n)], r_sm.at[pl.ds(slot * n, n)], isem.at[1, slot]))


def _issue_rows(ps_ref, e_sm, r_sm, slot, tm, copy_for):
    n = tm * TOP_K

    def body(it, c):
        toks = [it * ROW_DMA_UNROLL + u for u in range(ROW_DMA_UNROLL)]
        rows = []
        for tok in toks:
            for k in range(TOP_K):
                a = slot * n + tok * TOP_K + k
                rows.append(ps_ref[e_sm[a]] + r_sm[a])
        for u, tok in enumerate(toks):
            for k in range(TOP_K):
                copy_for(tok, k, rows[u * TOP_K + k]).start(priority=k % 2)
        return c

    lax.fori_loop(0, tm // ROW_DMA_UNROLL, body, 0)


def _dispatch_kernel(ps_ref, lo_ref, hi_ref, e_hbm, r_hbm, hn_ref, xin_out, e_sm, r_sm, zero_scr,
                     isem, sem, zsem):
    n = e_sm.shape[0] // 2
    tm = n // TOP_K
    i = pl.program_id(0)
    last = pl.num_programs(0) - 1
    slot = i % 2
    idx = functools.partial(_index_copies, e_hbm, r_hbm, e_sm, r_sm, isem)

    @pl.when(i == 0)
    def _():
        for c in idx(0, 0):
            c.start()

    for c in idx(i, slot):
        c.wait()

    @pl.when(i < last)
    def _():
        for c in idx(i + 1, 1 - slot):
            c.start()

    _issue_rows(ps_ref, e_sm, r_sm, slot, tm,
                lambda tok, k, row: pltpu.make_async_copy(_row(hn_ref, tok), _row(xin_out, row), sem))

    @pl.when(i == last)
    def _():
        zero_scr[...] = jnp.zeros(zero_scr.shape, F32)
        zrow = _row(zero_scr, 0)

        def per_span(s, c):
            def zissue(r, c2):
                pltpu.make_async_copy(zrow, _row(xin_out, r), zsem).start()
                return c2

            def zdrain(r, c2):
                pltpu.make_async_copy(zrow, _row(xin_out, r), zsem).wait()
                return c2

            lax.fori_loop(lo_ref[s], hi_ref[s], zissue, 0)
            lax.fori_loop(lo_ref[s], hi_ref[s], zdrain, 0)
            return c

        lax.fori_loop(0, lo_ref.shape[0], per_span, 0)

    for k in range(TOP_K):
        pltpu.make_async_copy(hn_ref, xin_out.at[pl.ds(0, tm * SUBLANES), :], sem).wait()


def dispatch(pstart, pad_lo, pad_hi, e_flat, r_flat, hn, n_rows, tm=1024):
    t = hn.shape[0] // SUBLANES
    grid_spec = pltpu.PrefetchScalarGridSpec(
        num_scalar_prefetch=3,
        grid=(t // tm,),
        in_specs=[pl.BlockSpec(memory_space=pl.ANY), pl.BlockSpec(memory_space=pl.ANY),
                  pl.BlockSpec((tm * SUBLANES, LANES), lambda i, ps, lo, hi: (i, 0))],
        out_specs=pl.BlockSpec(memory_space=pl.ANY),
        scratch_shapes=[pltpu.SMEM((2 * tm * TOP_K,), I32), pltpu.SMEM((2 * tm * TOP_K,), I32),
                        pltpu.VMEM((SUBLANES, LANES), F32),
                        pltpu.SemaphoreType.DMA((2, 2)), pltpu.SemaphoreType.DMA,
                        pltpu.SemaphoreType.DMA],
    )
    return pl.pallas_call(
        _dispatch_kernel,
        grid_spec=grid_spec,
        out_shape=jax.ShapeDtypeStruct((n_rows * SUBLANES, LANES), F32),
        compiler_params=_params(("arbitrary",)),
        name="dispatch",
    )(pstart, pad_lo, pad_hi, e_flat, r_flat, hn)


def _moe_kernel(be_ref, nu_ref, x_ref, wgu_ref, bgu_ref, wd_ref, bd_ref, o_ref, wgu_scr, wd_scr):
    i = pl.program_id(0)
    dff = wd_ref.shape[1]
    active = i < nu_ref[0]

    @pl.when(active & ((i == 0) | (be_ref[i] != be_ref[jnp.maximum(i - 1, 0)])))
    def _():
        for r in range(0, wgu_ref.shape[1], MOE_CAST_ROWS):
            wgu_scr[r:r + MOE_CAST_ROWS, :] = wgu_ref[0, r:r + MOE_CAST_ROWS, :].astype(BF16)
        for r in range(0, dff, MOE_CAST_ROWS):
            wd_scr[r:r + MOE_CAST_ROWS, :] = wd_ref[0, r:r + MOE_CAST_ROWS, :].astype(BF16)

    @pl.when(active)
    def _():
        x = _load_rows(x_ref, MOE_BLOCK).astype(BF16)
        gu = _dot(x, wgu_scr[...]) + bgu_ref[0]
        g_lin = jnp.minimum(gu[:, :dff], SWIGLU_LIMIT)
        u_lin = jnp.clip(gu[:, dff:], -SWIGLU_LIMIT, SWIGLU_LIMIT)
        act = g_lin * _sigmoid(SWIGLU_ALPHA * g_lin) * (u_lin + 1.0)
        _store_rows(o_ref, _dot(act.astype(BF16), wd_scr[...]) + bd_ref[0])

    @pl.when(jnp.logical_not(active))
    def _():
        o_ref[...] = jnp.zeros(o_ref.shape, F32)


def moe_ffn(blk_expert, n_used, xin, wgu, bgu, wd, bd):
    blk_rows = MOE_BLOCK * SUBLANES
    nb = xin.shape[0] // blk_rows
    d = wgu.shape[1]
    f2 = wgu.shape[2]
    dff = wd.shape[1]
    grid_spec = pltpu.PrefetchScalarGridSpec(
        num_scalar_prefetch=2,
        grid=(nb,),
        in_specs=[
            pl.BlockSpec((blk_rows, LANES), lambda i, be, nu: (jnp.minimum(i, nu[0] - 1), 0)),
            pl.BlockSpec((1, d, f2), lambda i, be, nu: (be[i], 0, 0)),
            pl.BlockSpec((1, 1, f2), lambda i, be, nu: (be[i], 0, 0)),
            pl.BlockSpec((1, dff, d), lambda i, be, nu: (be[i], 0, 0)),
            pl.BlockSpec((1, 1, d), lambda i, be, nu: (be[i], 0, 0)),
        ],
        out_specs=pl.BlockSpec((blk_rows, LANES), lambda i, be, nu: (i, 0)),
        scratch_shapes=[pltpu.VMEM((d, f2), BF16), pltpu.VMEM((dff, d), BF16)],
    )
    return pl.pallas_call(
        _moe_kernel,
        grid_spec=grid_spec,
        out_shape=jax.ShapeDtypeStruct(xin.shape, F32),
        compiler_params=_params(("arbitrary",)),
        name="moe_ffn",
    )(blk_expert, n_used, xin, wgu, bgu, wd, bd)


def _combine_kernel(ps_ref, e_hbm, r_hbm, yb_hbm, h2_ref, gate_ref, gf_ref, o_ref,
                    e_sm, r_sm, buf, isem, sem):
    tm = h2_ref.shape[0]
    i = pl.program_id(0)
    last = pl.num_programs(0) - 1
    slot = i % 2
    idx = functools.partial(_index_copies, e_hbm, r_hbm, e_sm, r_sm, isem)

    def gather(s):
        _issue_rows(ps_ref, e_sm, r_sm, s, tm,
                    lambda tok, k, row: pltpu.make_async_copy(_row(yb_hbm, row), _row(buf.at[s, k], tok),
                                                              sem.at[s]))

    @pl.when(i == 0)
    def _():
        for c in idx(0, 0):
            c.start()
        for c in idx(0, 0):
            c.wait()
        gather(0)

        @pl.when(last > 0)
        def _():
            for c in idx(1, 1):
                c.start()

    @pl.when(i < last)
    def _():
        for c in idx(i + 1, 1 - slot):
            c.wait()
        gather(1 - slot)

    @pl.when(i + 2 <= last)
    def _():
        for c in idx(i + 2, slot):
            c.start()

    for k in range(TOP_K):
        pltpu.make_async_copy(yb_hbm.at[pl.ds(0, tm * SUBLANES), :], buf.at[slot, k], sem.at[slot]).wait()

    gate = gate_ref[...]
    h3 = h2_ref[...]
    for k in range(TOP_K):
        h3 = h3 + gate[:, k:k + 1] * _load_rows(buf.at[slot, k], tm)
    o_ref[...] = h3 * lax.rsqrt(jnp.mean(h3 * h3, axis=-1, keepdims=True) + EPS) * gf_ref[...]


def combine(pstart, e_flat, r_flat, yb, h2, gate, gf, tm=512):
    t, d = h2.shape
    grid_spec = pltpu.PrefetchScalarGridSpec(
        num_scalar_prefetch=1,
        grid=(t // tm,),
        in_specs=[
            pl.BlockSpec(memory_space=pl.ANY),
            pl.BlockSpec(memory_space=pl.ANY),
            pl.BlockSpec(memory_space=pl.ANY),
            pl.BlockSpec((tm, d), lambda i, ps: (i, 0)),
            pl.BlockSpec((tm, LANES), lambda i, ps: (i, 0)),
            pl.BlockSpec((1, d), lambda i, ps: (0, 0)),
        ],
        out_specs=pl.BlockSpec((tm, d), lambda i, ps: (i, 0)),
        scratch_shapes=[pltpu.SMEM((2 * tm * TOP_K,), I32), pltpu.SMEM((2 * tm * TOP_K,), I32),
                        pltpu.VMEM((2, TOP_K, tm * SUBLANES, LANES), F32),
                        pltpu.SemaphoreType.DMA((2, 2)), pltpu.SemaphoreType.DMA((2,))],
    )
    return pl.pallas_call(
        _combine_kernel,
        grid_spec=grid_spec,
        out_shape=jax.ShapeDtypeStruct((t, d), F32),
        compiler_params=_params(("arbitrary",)),
        name="combine",
    )(pstart, e_flat, r_flat, yb, h2, gate, gf)


def _pad_lanes(v, fill=0.0):
    return jnp.pad(v.reshape(1, -1).astype(F32), ((0, 0), (0, LANES - v.shape[-1])), constant_values=fill)


def _layer(h, mem, p, merge_tm=512):
    bsz, seq, d = h.shape
    assert d == SUBLANES * LANES, "the row-tile layout needs one (8,128) tile per model row"
    t = bsz * seq
    mlen = mem.shape[1]
    inner = SSD_HEADS * SSD_HEAD_DIM
    conv_ch = inner + 2 * SSD_GROUPS * SSD_STATE
    sbi = SB_HEADS * SB_HEAD_DIM
    n_experts = p["w_router"].shape[1]
    x2 = h.reshape(t, d)

    w_in = p["w_in"]
    dt0 = inner + conv_ch
    w_main = jnp.concatenate([w_in[:, :dt0], w_in[:, dt0 + SSD_HEADS:]], axis=1).astype(BF16)
    w_dt = jnp.pad(w_in[:, dt0:dt0 + SSD_HEADS], ((0, 0), (0, LANES - SSD_HEADS))).astype(BF16)
    proj, dt_raw = inproj(x2, p["g_mix"].reshape(1, d), w_main, w_dt)
    qcol = dt0 // LANES
    kcol = qcol + sbi // LANES
    vcol = kcol + sbi // LANES
    gcol = (dt0 + 3 * sbi) // d

    heads = jnp.arange(LANES)[:, None]
    chan = jnp.arange(inner)[None, :]
    e_mat = (chan // SSD_HEAD_DIM == heads).astype(BF16)
    y_ssd = ssd(proj, dt_raw, p["conv_w"], p["conv_b"].reshape(1, conv_ch), _pad_lanes(p["dt_bias"]),
                _pad_lanes(p["a_log"]), jnp.repeat(p["d_skip"], SSD_HEAD_DIM).reshape(1, inner),
                p["g_ssd"].reshape(1, inner), e_mat, bsz, seq)

    jj = jnp.arange(SB_T)[:, None]
    ss = jnp.arange(SB_T + LANES)[None, :]
    u_mat = ((ss >= SB_T) | (jj > ss)).astype(BF16)
    y_sb = sb_attn(proj, u_mat, bsz, seq, qcol, kcol, vcol)

    kv = mem_kv(mem.reshape(bsz * mlen, d), p["g_mem"].reshape(1, d), p["w_xkv"].astype(BF16), mlen)

    wr = jnp.pad(p["w_router"], ((0, 0), (0, LANES - n_experts)))
    wrh = wr.astype(BF16)
    wrl = (wr - wrh.astype(F32)).astype(BF16)
    tri = (jnp.arange(merge_tm)[None, :] <= jnp.arange(merge_tm)[:, None]).astype(BF16)
    h2, hn, eidx, rank, gate, counts = merge(
        x2, y_ssd, y_sb, proj, gcol, p["w_ssd_out"].astype(BF16), p["w_sb_out"].astype(BF16),
        p["w_out"].astype(BF16), p["g_xattn"].reshape(1, d), p["w_xq"].astype(BF16), kv,
        p["w_xo"].astype(BF16), p["g_moe"].reshape(1, d), wrh, wrl,
        _pad_lanes(p["b_router"], NEG_BIG), tri, seq, mlen, merge_tm)

    cnt = counts[0, :n_experts]
    padded = (cnt + MOE_BLOCK - 1) // MOE_BLOCK * MOE_BLOCK
    pend = jnp.cumsum(padded)
    pstart = (pend - padded).astype(I32)
    n_blocks = -(-(t * TOP_K) // MOE_BLOCK) + n_experts
    n_rows = n_blocks * MOE_BLOCK
    n_used = (pend[-1] // MOE_BLOCK).astype(I32)
    blk = jnp.minimum(jnp.arange(n_blocks, dtype=I32), n_used - 1) * MOE_BLOCK
    blk_expert = jnp.sum(blk[:, None] >= pend[None, :n_experts - 1], axis=1).astype(I32)
    e_flat = eidx[:, :TOP_K].reshape(-1)
    r_flat = rank[:, :TOP_K].reshape(-1)
    pad_lo = jnp.concatenate([pstart + cnt, pend[-1:]]).astype(I32)
    pad_hi = jnp.concatenate([pend, jnp.full((1,), n_rows)]).astype(I32)

    xin = dispatch(pstart, pad_lo, pad_hi, e_flat, r_flat, hn, n_rows)
    yb = moe_ffn(blk_expert, n_used.reshape(1), xin, p["w_gu"], p["b_gu"][:, None, :],
                 p["w_down"], p["b_down"][:, None, :])
    return h2, (pstart, e_flat, r_flat, yb, gate)


def kernel(x, mem, g_mix, w_in, conv_w, conv_b, dt_bias, a_log, d_skip, g_ssd, w_ssd_out, w_sb_out,
           w_out, g_xattn, g_mem, w_xq, w_xkv, w_xo, g_moe, w_router, b_router, w_gu, b_gu, w_down,
           b_down, g_final):
    stacked = dict(g_mix=g_mix, w_in=w_in, conv_w=conv_w, conv_b=conv_b, dt_bias=dt_bias, a_log=a_log,
                   d_skip=d_skip, g_ssd=g_ssd, w_ssd_out=w_ssd_out, w_sb_out=w_sb_out, w_out=w_out,
                   g_xattn=g_xattn, g_mem=g_mem, w_xq=w_xq, w_xkv=w_xkv, w_xo=w_xo, g_moe=g_moe,
                   w_router=w_router, b_router=b_router, w_gu=w_gu, b_gu=b_gu, w_down=w_down,
                   b_down=b_down)
    assert g_mix.shape[0] == 1, "stacked depth other than 1 is not supported"
    bsz, seq, d = x.shape
    p = {k: v[0] for k, v in stacked.items()}
    h2, (pstart, e_flat, r_flat, yb, gate) = _layer(x, mem, p)
    out = combine(pstart, e_flat, r_flat, yb, h2, gate, g_final.reshape(1, d))
    return out.reshape(bsz, seq, d)
```

```python
import functools

import jax
import jax.numpy as jnp
from jax import lax
from jax.experimental import pallas as pl
from jax.experimental.pallas import tpu as pltpu

F32 = jnp.float32
BF16 = jnp.bfloat16
I32 = jnp.int32

LANES = 128
SUBLANES = 8
EPS = 1e-6
VMEM_LIMIT = 56 * 1024 * 1024

SSD_HEADS = 16
SSD_HEAD_DIM = 64
SSD_GROUPS = 2
SSD_STATE = 128
SSD_CONV = 4
SB_HEADS = 8
SB_HEAD_DIM = 64
XA_HEADS = 4
XA_HEAD_DIM = 128
TOP_K = 4
SWIGLU_LIMIT = 7.0
SWIGLU_ALPHA = 1.702

INPROJ_TN = 1024
SSD_L = 128
SB_T = 256
SB_SKIP = 110.0
MOE_BLOCK = 512
MOE_CAST_ROWS = 128
ROW_DMA_UNROLL = 2
NEG_BIG = -1e30


def _dot(a, b):
    return jnp.dot(a, b, preferred_element_type=F32)


def _dot_nt(a, b):
    return lax.dot_general(a, b, (((1,), (1,)), ((), ())), preferred_element_type=F32)


def _split_bf16(v):
    hi = v.astype(BF16)
    lo = (v - hi.astype(F32)).astype(BF16)
    return hi, lo


def _dot2(v, m):
    hi, lo = _split_bf16(v)
    return _dot(hi, m) + _dot(lo, m)


def _softplus(x):
    return jnp.maximum(x, 0.0) + jnp.log(1.0 + jnp.exp(-jnp.abs(x)))


def _sigmoid(x):
    return 1.0 / (1.0 + jnp.exp(-x))


def _params(sem, vmem=VMEM_LIMIT):
    return pltpu.CompilerParams(dimension_semantics=sem, vmem_limit_bytes=vmem)


def _store_rows(ref, val):
    n = val.shape[0]
    for s in range(SUBLANES):
        ref[pl.ds(s, n, stride=SUBLANES), :] = val[:, s * LANES:(s + 1) * LANES]


def _load_rows(ref, n):
    return jnp.concatenate([ref[pl.ds(s, n, stride=SUBLANES), :] for s in range(SUBLANES)], axis=1)


def _row(ref, r):
    return ref.at[pl.ds(pl.multiple_of(r * SUBLANES, SUBLANES), SUBLANES), :]


def _inproj_kernel(x_ref, g_ref, w_ref, wdt_ref, proj_ref, dt_ref):
    x = x_ref[...]
    n = x * lax.rsqrt(jnp.mean(x * x, axis=-1, keepdims=True) + EPS) * g_ref[...]
    nb = n.astype(BF16)
    dt_ref[...] = _dot(nb, wdt_ref[...])
    for c in range(0, w_ref.shape[1], INPROJ_TN):
        proj_ref[:, c:c + INPROJ_TN] = _dot(nb, w_ref[:, c:c + INPROJ_TN]).astype(BF16)


def inproj(x2, g, w, wdt, tm=512):
    t, d = x2.shape
    n = w.shape[1]
    return pl.pallas_call(
        _inproj_kernel,
        grid=(t // tm,),
        in_specs=[
            pl.BlockSpec((tm, d), lambda i: (i, 0)),
            pl.BlockSpec((1, d), lambda i: (0, 0)),
            pl.BlockSpec((d, n), lambda i: (0, 0), pipeline_mode=pl.Buffered(1)),
            pl.BlockSpec((d, LANES), lambda i: (0, 0)),
        ],
        out_specs=[
            pl.BlockSpec((tm, n), lambda i: (i, 0)),
            pl.BlockSpec((tm, LANES), lambda i: (i, 0)),
        ],
        out_shape=[
            jax.ShapeDtypeStruct((t, n), BF16),
            jax.ShapeDtypeStruct((t, LANES), F32),
        ],
        compiler_params=_params(("arbitrary",)),
        name="inproj",
    )(x2, g, w, wdt)


def _ssd_kernel(z_ref, xs_ref, bc_ref, dt_ref, cw_ref, cb_ref, dtb_ref, alog_ref, dskip_ref,
                gn_ref, e_ref, o_ref, ext_scr, state_scr):
    L = SSD_L
    inner = SSD_HEADS * SSD_HEAD_DIM
    gw = inner // SSD_GROUPS
    n = SSD_STATE

    @pl.when(pl.program_id(1) == 0)
    def _():
        ext_scr[0:SUBLANES, :] = jnp.zeros((SUBLANES, ext_scr.shape[1]), F32)
        state_scr[...] = jnp.zeros(state_scr.shape, F32)

    ext_scr[SUBLANES:SUBLANES + L, 0:inner] = xs_ref[...].astype(F32)
    ext_scr[SUBLANES:SUBLANES + L, inner:] = bc_ref[...].astype(F32)
    conv = cb_ref[...]
    for k in range(SSD_CONV):
        off = SUBLANES - (SSD_CONV - 1) + k
        conv = conv + cw_ref[k:k + 1, :] * ext_scr[off:off + L, :]
    ext_scr[0:SUBLANES, :] = ext_scr[L:L + SUBLANES, :]
    xbc = conv * _sigmoid(conv)
    xs = xbc[:, :inner]

    dt = _softplus(dt_ref[...] + dtb_ref[...])
    da = dt * (-jnp.exp(alog_ref[...]))
    row = lax.broadcasted_iota(I32, (L, L), 0)
    col = lax.broadcasted_iota(I32, (L, L), 1)
    tril = col <= row
    cs = jnp.dot(tril.astype(F32), da, precision=lax.Precision.HIGHEST,
                 preferred_element_type=F32)
    cst = cs.T
    cs_last = cs[L - 1:L, :]

    e = e_ref[...]
    xdt = xs * _dot2(dt, e)
    xw = (xdt * _dot2(jnp.exp(cs_last - cs), e)).astype(BF16)
    od_e = _dot2(jnp.exp(cs), e)
    xdt_b = xdt.astype(BF16)
    lane = lax.broadcasted_iota(I32, (L, LANES), 1)

    y_parts = []
    new_states = []
    for g in range(SSD_GROUPS):
        bg = xbc[:, inner + g * n:inner + (g + 1) * n]
        cg = xbc[:, inner + SSD_GROUPS * n + g * n:inner + SSD_GROUPS * n + (g + 1) * n]
        bg_b = bg.astype(BF16)
        cg_b = cg.astype(BF16)
        cbm = _dot_nt(cg_b, bg_b)
        st = state_scr[:, g * gw:(g + 1) * gw]
        y_off = _dot(cg_b, st.astype(BF16))
        new_states.append(_dot(bg.T.astype(BF16), xw[:, g * gw:(g + 1) * gw]))
        heads_per_group = SSD_HEADS // SSD_GROUPS
        for hp in range(heads_per_group // 2):
            pair = xdt_b[:, g * gw + hp * LANES:g * gw + (hp + 1) * LANES]
            yd = []
            for sub in range(2):
                h = g * heads_per_group + hp * 2 + sub
                seg = cs[:, h:h + 1] - cst[h:h + 1, :]
                lm = jnp.exp(jnp.where(tril, seg, -jnp.inf))
                yd.append(_dot((cbm * lm).astype(BF16), pair))
            y_parts.append((jnp.where(lane < SSD_HEAD_DIM, yd[0], yd[1]), g, hp, y_off))

    cols = []
    for (yd, g, hp, y_off) in y_parts:
        c0 = g * gw + hp * LANES
        cols.append(yd + y_off[:, hp * LANES:(hp + 1) * LANES] * od_e[:, c0:c0 + LANES])
    y = jnp.concatenate(cols, axis=1) + xs * dskip_ref[...]
    zf = z_ref[...].astype(F32)
    y = y * (zf * _sigmoid(zf))
    outs = []
    for g in range(SSD_GROUPS):
        yg = y[:, g * gw:(g + 1) * gw]
        ms = jnp.mean(yg * yg, axis=-1, keepdims=True)
        outs.append(yg * lax.rsqrt(ms + EPS) * gn_ref[:, g * gw:(g + 1) * gw])
    o_ref[...] = jnp.concatenate(outs, axis=1).astype(BF16)

    cd_e = od_e[L - 1:L, :]
    for g in range(SSD_GROUPS):
        state_scr[:, g * gw:(g + 1) * gw] = (
            state_scr[:, g * gw:(g + 1) * gw] * cd_e[:, g * gw:(g + 1) * gw] + new_states[g])


def ssd(proj, dt_raw, conv_w, conv_b, dtb, alog, dskip_e, gn, e_mat, bsz, seq):
    L = SSD_L
    inner = SSD_HEADS * SSD_HEAD_DIM
    bcw = 2 * SSD_GROUPS * SSD_STATE
    nc = seq // L
    ch = inner + bcw
    full = lambda shape: pl.BlockSpec(shape, lambda b, c: (0,) * len(shape))
    return pl.pallas_call(
        _ssd_kernel,
        grid=(bsz, nc),
        in_specs=[
            pl.BlockSpec((L, inner), lambda b, c: (b * nc + c, 0)),
            pl.BlockSpec((L, inner), lambda b, c: (b * nc + c, 1)),
            pl.BlockSpec((L, bcw), lambda b, c: (b * nc + c, 2 * inner // bcw)),
            pl.BlockSpec((L, LANES), lambda b, c: (b * nc + c, 0)),
            full((SSD_CONV, ch)), full((1, ch)), full((1, LANES)), full((1, LANES)),
            full((1, inner)), full((1, inner)), full((LANES, inner)),
        ],
        out_specs=pl.BlockSpec((L, inner), lambda b, c: (b * nc + c, 0)),
        out_shape=jax.ShapeDtypeStruct((bsz * seq, inner), BF16),
        scratch_shapes=[pltpu.VMEM((L + 2 * SUBLANES, ch), F32), pltpu.VMEM((SSD_STATE, inner), F32)],
        compiler_params=_params(("arbitrary", "arbitrary")),
        name="ssd",
    )(proj, proj, proj, dt_raw, conv_w, conv_b, dtb, alog, dskip_e, gn, e_mat)


def _sb_kernel(q_ref, k_ref, v_ref, u_ref, o_ref, carry_scr, acc_scr):
    t = SB_T
    nh = LANES // SB_HEAD_DIM
    i = pl.program_id(2)
    lane = lax.broadcasted_iota(I32, (t, LANES), 1)
    q = q_ref[...]
    scale = SB_HEAD_DIM ** -0.5
    qh = []
    for h in range(nh):
        in_head = (lane >= h * SB_HEAD_DIM) & (lane < (h + 1) * SB_HEAD_DIM)
        qh.append((jnp.where(in_head, q, jnp.zeros_like(q)).astype(F32) * scale).astype(BF16))
    u = u_ref[...]

    def step(j, diagonal, state):
        ks = pl.multiple_of(j * t, t)
        kb = k_ref[pl.ds(ks, t), :]
        vb = v_ref[pl.ds(ks, t), :]
        if diagonal:
            mask = lax.broadcasted_iota(I32, (t, t), 1) < lax.broadcasted_iota(I32, (t, t), 0)
        out = []
        for h in range(nh):
            x = _dot_nt(qh[h], kb)
            sp = _softplus(x)
            if diagonal:
                sp = jnp.where(mask, sp, 0.0)
            suf = _dot(sp.astype(BF16), u)
            e = x - sp - suf[:, :t]
            if state is not None:
                e = e - jnp.concatenate([state[h][0]] * (t // LANES), axis=1)
            w = jnp.exp(e)
            if diagonal:
                w = jnp.where(mask, w, 0.0)
            pv = _dot(w.astype(BF16), vb)
            if state is None:
                out.append((suf[:, t:], pv))
            else:
                out.append((state[h][0] + suf[:, t:], state[h][1] + pv))
        return out

    def smallest(state):
        return functools.reduce(jnp.minimum, [jnp.min(c) for c, _ in state])

    def save(state):
        for h in range(nh):
            carry_scr[h] = state[h][0]
            acc_scr[h] = state[h][1]

    def cond(st):
        j, go = st
        return (j >= 0) & (go > 0)

    def body(st):
        j, _ = st
        state = step(j, False, [(carry_scr[h], acc_scr[h]) for h in range(nh)])
        save(state)
        return j - 1, (smallest(state) < SB_SKIP).astype(I32)

    @pl.when(i == 0)
    def _():
        save(step(0, True, None))

    @pl.when(i > 0)
    def _():
        state = step(i - 1, False, step(i, True, None))
        save(state)
        lax.while_loop(cond, body, (i - 2, (smallest(state) < SB_SKIP).astype(I32)))

    o_ref[...] = jnp.where(lane < SB_HEAD_DIM, acc_scr[0], acc_scr[1]).astype(BF16)


def sb_attn(proj, u_mat, bsz, seq, qcol, kcol, vcol):
    t = SB_T
    nq = seq // t
    npairs = SB_HEADS * SB_HEAD_DIM // LANES
    nh = LANES // SB_HEAD_DIM
    return pl.pallas_call(
        _sb_kernel,
        grid=(bsz, npairs, nq),
        in_specs=[
            pl.BlockSpec((t, LANES), lambda b, p, i: (b * nq + i, qcol + p)),
            pl.BlockSpec((seq, LANES), lambda b, p, i: (b, kcol + p)),
            pl.BlockSpec((seq, LANES), lambda b, p, i: (b, vcol + p)),
            pl.BlockSpec((t, t + LANES), lambda b, p, i: (0, 0)),
        ],
        out_specs=pl.BlockSpec((t, LANES), lambda b, p, i: (b * nq + i, p)),
        out_shape=jax.ShapeDtypeStruct((bsz * seq, npairs * LANES), BF16),
        scratch_shapes=[pltpu.VMEM((nh, t, LANES), F32), pltpu.VMEM((nh, t, LANES), F32)],
        compiler_params=_params(("arbitrary", "arbitrary", "arbitrary")),
        name="sb_attn",
    )(proj, proj, proj, u_mat)


def _memkv_kernel(m_ref, g_ref, w_ref, o_ref):
    x = m_ref[...]
    n = x * lax.rsqrt(jnp.mean(x * x, axis=-1, keepdims=True) + EPS) * g_ref[...]
    o_ref[...] = _dot(n.astype(BF16), w_ref[...]).astype(BF16)


def mem_kv(mem2, g, w, mlen):
    rows, d = mem2.shape
    n = w.shape[1]
    return pl.pallas_call(
        _memkv_kernel,
        grid=(rows // mlen,),
        in_specs=[pl.BlockSpec((mlen, d), lambda b: (b, 0)),
                  pl.BlockSpec((1, d), lambda b: (0, 0)),
                  pl.BlockSpec((d, n), lambda b: (0, 0))],
        out_specs=pl.BlockSpec((mlen, n), lambda b: (b, 0)),
        out_shape=jax.ShapeDtypeStruct((rows, n), BF16),
        compiler_params=_params(("arbitrary",)),
        name="mem_kv",
    )(mem2, g, w)


def _merge_kernel(x_ref, yssd_ref, ysb_ref, gssd_ref, gsb_ref, wso_ref, wsbo_ref, wo_ref, gx_ref,
                  wxq_ref, kv_ref, wxo_ref, gm_ref, wrh_ref, wrl_ref, br_ref, tri_ref,
                  h2_ref, hn_ref, eidx_ref, rank_ref, gate_ref, cnt_ref, cnt_scr):
    @pl.when(pl.program_id(0) == 0)
    def _():
        cnt_scr[...] = jnp.zeros(cnt_scr.shape, F32)

    tm = x_ref.shape[0]
    y_ssd = _dot(yssd_ref[...], wso_ref[...])
    y_sb = _dot(ysb_ref[...], wsbo_ref[...])
    merged = (_sigmoid(gssd_ref[...].astype(F32)) * y_ssd
              + _sigmoid(gsb_ref[...].astype(F32)) * y_sb)
    h1 = x_ref[...] + _dot(merged.astype(BF16), wo_ref[...])

    n2 = h1 * lax.rsqrt(jnp.mean(h1 * h1, axis=-1, keepdims=True) + EPS) * gx_ref[...]
    q = (_dot(n2.astype(BF16), wxq_ref[...]) * (XA_HEAD_DIM ** -0.5)).astype(BF16)
    xi = XA_HEADS * XA_HEAD_DIM
    heads = []
    for h in range(XA_HEADS):
        kh = kv_ref[:, h * XA_HEAD_DIM:(h + 1) * XA_HEAD_DIM]
        vh = kv_ref[:, xi + h * XA_HEAD_DIM:xi + (h + 1) * XA_HEAD_DIM]
        lg = _dot_nt(q[:, h * XA_HEAD_DIM:(h + 1) * XA_HEAD_DIM], kh)
        ex = jnp.exp(lg - jnp.max(lg, axis=-1, keepdims=True))
        den = jnp.sum(ex, axis=-1, keepdims=True)
        heads.append(_dot(ex.astype(BF16), vh) / den)
    o = jnp.concatenate(heads, axis=1).astype(BF16)
    h2 = h1 + _dot(o, wxo_ref[...])
    h2_ref[...] = h2

    n3 = h2 * lax.rsqrt(jnp.mean(h2 * h2, axis=-1, keepdims=True) + EPS) * gm_ref[...]
    _store_rows(hn_ref, n3)

    nh, nl = _split_bf16(n3)
    lg = _dot(nh, wrh_ref[...]) + _dot(nl, wrh_ref[...]) + _dot(nh, wrl_ref[...]) + br_ref[...]
    lane = lax.broadcasted_iota(I32, (tm, LANES), 1)
    lanef = lane.astype(F32)
    work = lg
    vals, idxs = [], []
    for _ in range(TOP_K):
        m = jnp.max(work, axis=-1, keepdims=True)
        idx = jnp.min(jnp.where(work == m, lanef, float(LANES)), axis=-1, keepdims=True)
        vals.append(m)
        idxs.append(idx)
        work = jnp.where(lanef == idx, NEG_BIG * 2.0, work)
    exps = [jnp.exp(v - vals[0]) for v in vals]
    den = exps[0]
    for ex in exps[1:]:
        den = den + ex

    sel = jnp.zeros((tm, LANES), F32)
    for idx in idxs:
        sel = sel + (lanef == idx).astype(F32)
    incl = _dot(tri_ref[...], sel.astype(BF16))
    before = incl - sel + cnt_scr[...]
    eidx = jnp.zeros((tm, LANES), F32)
    rank = jnp.zeros((tm, LANES), F32)
    gate = jnp.zeros((tm, LANES), F32)
    for k in range(TOP_K):
        rk = jnp.sum(jnp.where(lanef == idxs[k], before, 0.0), axis=-1, keepdims=True)
        eidx = jnp.where(lane == k, idxs[k], eidx)
        rank = jnp.where(lane == k, rk, rank)
        gate = jnp.where(lane == k, exps[k] / den, gate)
    eidx_ref[...] = eidx.astype(I32)
    rank_ref[...] = rank.astype(I32)
    gate_ref[...] = gate
    cnt_scr[...] = cnt_scr[...] + incl[tm - 1:tm, :]
    cnt_ref[...] = cnt_scr[...].astype(I32)


def merge(x2, y_ssd, y_sb, proj, gcol, wso, wsbo, wo, gx, wxq, kv, wxo, gm, wrh, wrl, br, tri,
          seq, mlen, tm):
    t, d = x2.shape
    si = y_ssd.shape[1]
    sbi = y_sb.shape[1]
    xi2 = kv.shape[1]
    full = lambda a: pl.BlockSpec(a.shape, lambda i: (0,) * a.ndim)
    row = lambda w: pl.BlockSpec((tm, w), lambda i: (i, 0))
    return pl.pallas_call(
        _merge_kernel,
        grid=(t // tm,),
        in_specs=[
            row(d), row(si), row(sbi),
            pl.BlockSpec((tm, d), lambda i: (i, gcol)),
            pl.BlockSpec((tm, d), lambda i: (i, gcol + 1)),
            full(wso), full(wsbo), full(wo), full(gx), full(wxq),
            pl.BlockSpec((mlen, xi2), lambda i: ((i * tm) // seq, 0)),
            full(wxo), full(gm), full(wrh), full(wrl), full(br), full(tri),
        ],
        out_specs=[
            row(d),
            pl.BlockSpec((tm * d // LANES, LANES), lambda i: (i, 0)),
            row(LANES), row(LANES), row(LANES),
            pl.BlockSpec((1, LANES), lambda i: (0, 0)),
        ],
        out_shape=[
            jax.ShapeDtypeStruct((t, d), F32),
            jax.ShapeDtypeStruct((t * d // LANES, LANES), F32),
            jax.ShapeDtypeStruct((t, LANES), I32),
            jax.ShapeDtypeStruct((t, LANES), I32),
            jax.ShapeDtypeStruct((t, LANES), F32),
            jax.ShapeDtypeStruct((1, LANES), I32),
        ],
        scratch_shapes=[pltpu.VMEM((1, LANES), F32)],
        compiler_params=_params(("arbitrary",)),
        name="merge",
    )(x2, y_ssd, y_sb, proj, proj, wso, wsbo, wo, gx, wxq, kv, wxo, gm, wrh, wrl, br, tri)


def _index_copies(e_hbm, r_hbm, e_sm, r_sm, isem, step, slot):
    n = e_sm.shape[0] // 2
    return (pltpu.make_async_copy(e_hbm.at[pl.ds(step * n, n)], e_sm.at[pl.ds(slot * n, n)], isem.at[0, slot]),
            pltpu.make_async_copy(r_hbm.at[pl.ds(step * n, n)], r_sm.at[pl.ds(slot * n, n)], isem.at[1, slot]))


def _issue_rows(ps_ref, e_sm, r_sm, slot, tm, copy_for):
    n = tm * TOP_K

    def body(it, c):
        toks = [it * ROW_DMA_UNROLL + u for u in range(ROW_DMA_UNROLL)]
        rows = []
        for tok in toks:
            for k in range(TOP_K):
                a = slot * n + tok * TOP_K + k
                rows.append(ps_ref[e_sm[a]] + r_sm[a])
        for u, tok in enumerate(toks):
            for k in range(TOP_K):
                copy_for(tok, k, rows[u * TOP_K + k]).start(priority=k % 2)
        return c

    lax.fori_loop(0, tm // ROW_DMA_UNROLL, body, 0)


def _dispatch_kernel(ps_ref, lo_ref, hi_ref, e_hbm, r_hbm, hn_ref, xin_out, e_sm, r_sm, zero_scr,
                     isem, sem, zsem):
    n = e_sm.shape[0] // 2
    tm = n // TOP_K
    i = pl.program_id(0)
    last = pl.num_programs(0) - 1
    slot = i % 2
    idx = functools.partial(_index_copies, e_hbm, r_hbm, e_sm, r_sm, isem)

    @pl.when(i == 0)
    def _():
        for c in idx(0, 0):
            c.start()

    for c in idx(i, slot):
        c.wait()

    @pl.when(i < last)
    def _():
        for c in idx(i + 1, 1 - slot):
            c.start()

    _issue_rows(ps_ref, e_sm, r_sm, slot, tm,
                lambda tok, k, row: pltpu.make_async_copy(_row(hn_ref, tok), _row(xin_out, row), sem))

    @pl.when(i == last)
    def _():
        zero_scr[...] = jnp.zeros(zero_scr.shape, F32)
        zrow = _row(zero_scr, 0)

        def per_span(s, c):
            def zissue(r, c2):
                pltpu.make_async_copy(zrow, _row(xin_out, r), zsem).start()
                return c2

            def zdrain(r, c2):
                pltpu.make_async_copy(zrow, _row(xin_out, r), zsem).wait()
                return c2

            lax.fori_loop(lo_ref[s], hi_ref[s], zissue, 0)
            lax.fori_loop(lo_ref[s], hi_ref[s], zdrain, 0)
            return c

        lax.fori_loop(0, lo_ref.shape[0], per_span, 0)

    for k in range(TOP_K):
        pltpu.make_async_copy(hn_ref, xin_out.at[pl.ds(0, tm * SUBLANES), :], sem).wait()


def dispatch(pstart, pad_lo, pad_hi, e_flat, r_flat, hn, n_rows, tm=1024):
    t = hn.shape[0] // SUBLANES
    grid_spec = pltpu.PrefetchScalarGridSpec(
        num_scalar_prefetch=3,
        grid=(t // tm,),
        in_specs=[pl.BlockSpec(memory_space=pl.ANY), pl.BlockSpec(memory_space=pl.ANY),
                  pl.BlockSpec((tm * SUBLANES, LANES), lambda i, ps, lo, hi: (i, 0))],
        out_specs=pl.BlockSpec(memory_space=pl.ANY),
        scratch_shapes=[pltpu.SMEM((2 * tm * TOP_K,), I32), pltpu.SMEM((2 * tm * TOP_K,), I32),
                        pltpu.VMEM((SUBLANES, LANES), F32),
                        pltpu.SemaphoreType.DMA((2, 2)), pltpu.SemaphoreType.DMA,
                        pltpu.SemaphoreType.DMA],
    )
    return pl.pallas_call(
        _dispatch_kernel,
        grid_spec=grid_spec,
        out_shape=jax.ShapeDtypeStruct((n_rows * SUBLANES, LANES), F32),
        compiler_params=_params(("arbitrary",)),
        name="dispatch",
    )(pstart, pad_lo, pad_hi, e_flat, r_flat, hn)


def _moe_kernel(be_ref, nu_ref, x_ref, wgu_ref, bgu_ref, wd_ref, bd_ref, o_ref, wgu_scr, wd_scr):
    i = pl.program_id(0)
    dff = wd_ref.shape[1]
    active = i < nu_ref[0]

    @pl.when(active & ((i == 0) | (be_ref[i] != be_ref[jnp.maximum(i - 1, 0)])))
    def _():
        for r in range(0, wgu_ref.shape[1], MOE_CAST_ROWS):
            wgu_scr[r:r + MOE_CAST_ROWS, :] = wgu_ref[0, r:r + MOE_CAST_ROWS, :].astype(BF16)
        for r in range(0, dff, MOE_CAST_ROWS):
            wd_scr[r:r + MOE_CAST_ROWS, :] = wd_ref[0, r:r + MOE_CAST_ROWS, :].astype(BF16)

    @pl.when(active)
    def _():
        x = _load_rows(x_ref, MOE_BLOCK).astype(BF16)
        gu = _dot(x, wgu_scr[...]) + bgu_ref[0]
        g_lin = jnp.minimum(gu[:, :dff], SWIGLU_LIMIT)
        u_lin = jnp.clip(gu[:, dff:], -SWIGLU_LIMIT, SWIGLU_LIMIT)
        act = g_lin * _sigmoid(SWIGLU_ALPHA * g_lin) * (u_lin + 1.0)
        _store_rows(o_ref, _dot(act.astype(BF16), wd_scr[...]) + bd_ref[0])

    @pl.when(jnp.logical_not(active))
    def _():
        o_ref[...] = jnp.zeros(o_ref.shape, F32)


def moe_ffn(blk_expert, n_used, xin, wgu, bgu, wd, bd):
    blk_rows = MOE_BLOCK * SUBLANES
    nb = xin.shape[0] // blk_rows
    d = wgu.shape[1]
    f2 = wgu.shape[2]
    dff = wd.shape[1]
    grid_spec = pltpu.PrefetchScalarGridSpec(
        num_scalar_prefetch=2,
        grid=(nb,),
        in_specs=[
            pl.BlockSpec((blk_rows, LANES), lambda i, be, nu: (jnp.minimum(i, nu[0] - 1), 0)),
            pl.BlockSpec((1, d, f2), lambda i, be, nu: (be[i], 0, 0)),
            pl.BlockSpec((1, 1, f2), lambda i, be, nu: (be[i], 0, 0)),
            pl.BlockSpec((1, dff, d), lambda i, be, nu: (be[i], 0, 0)),
            pl.BlockSpec((1, 1, d), lambda i, be, nu: (be[i], 0, 0)),
        ],
        out_specs=pl.BlockSpec((blk_rows, LANES), lambda i, be, nu: (i, 0)),
        scratch_shapes=[pltpu.VMEM((d, f2), BF16), pltpu.VMEM((dff, d), BF16)],
    )
    return pl.pallas_call(
        _moe_kernel,
        grid_spec=grid_spec,
        out_shape=jax.ShapeDtypeStruct(xin.shape, F32),
        compiler_params=_params(("arbitrary",)),
        name="moe_ffn",
    )(blk_expert, n_used, xin, wgu, bgu, wd, bd)


def _combine_kernel(ps_ref, e_hbm, r_hbm, yb_hbm, h2_ref, gate_ref, gf_ref, o_ref,
                    e_sm, r_sm, buf, isem, sem):
    tm = h2_ref.shape[0]
    i = pl.program_id(0)
    last = pl.num_programs(0) - 1
    slot = i % 2
    idx = functools.partial(_index_copies, e_hbm, r_hbm, e_sm, r_sm, isem)

    def gather(s):
        _issue_rows(ps_ref, e_sm, r_sm, s, tm,
                    lambda tok, k, row: pltpu.make_async_copy(_row(yb_hbm, row), _row(buf.at[s, k], tok),
                                                              sem.at[s]))

    @pl.when(i == 0)
    def _():
        for c in idx(0, 0):
            c.start()
        for c in idx(0, 0):
            c.wait()
        gather(0)

        @pl.when(last > 0)
        def _():
            for c in idx(1, 1):
                c.start()

    @pl.when(i < last)
    def _():
        for c in idx(i + 1, 1 - slot):
            c.wait()
        gather(1 - slot)

    @pl.when(i + 2 <= last)
    def _():
        for c in idx(i + 2, slot):
            c.start()

    for k in range(TOP_K):
        pltpu.make_async_copy(yb_hbm.at[pl.ds(0, tm * SUBLANES), :], buf.at[slot, k], sem.at[slot]).wait()

    gate = gate_ref[...]
    h3 = h2_ref[...]
    for k in range(TOP_K):
        h3 = h3 + gate[:, k:k + 1] * _load_rows(buf.at[slot, k], tm)
    o_ref[...] = h3 * lax.rsqrt(jnp.mean(h3 * h3, axis=-1, keepdims=True) + EPS) * gf_ref[...]


def combine(pstart, e_flat, r_flat, yb, h2, gate, gf, tm=512):
    t, d = h2.shape
    grid_spec = pltpu.PrefetchScalarGridSpec(
        num_scalar_prefetch=1,
        grid=(t // tm,),
        in_specs=[
            pl.BlockSpec(memory_space=pl.ANY),
            pl.BlockSpec(memory_space=pl.ANY),
            pl.BlockSpec(memory_space=pl.ANY),
            pl.BlockSpec((tm, d), lambda i, ps: (i, 0)),
            pl.BlockSpec((tm, LANES), lambda i, ps: (i, 0)),
            pl.BlockSpec((1, d), lambda i, ps: (0, 0)),
        ],
        out_specs=pl.BlockSpec((tm, d), lambda i, ps: (i, 0)),
        scratch_shapes=[pltpu.SMEM((2 * tm * TOP_K,), I32), pltpu.SMEM((2 * tm * TOP_K,), I32),
                        pltpu.VMEM((2, TOP_K, tm * SUBLANES, LANES), F32),
                        pltpu.SemaphoreType.DMA((2, 2)), pltpu.SemaphoreType.DMA((2,))],
    )
    return pl.pallas_call(
        _combine_kernel,
        grid_spec=grid_spec,
        out_shape=jax.ShapeDtypeStruct((t, d), F32),
        compiler_params=_params(("arbitrary",)),
        name="combine",
    )(pstart, e_flat, r_flat, yb, h2, gate, gf)


def _pad_lanes(v, fill=0.0):
    return jnp.pad(v.reshape(1, -1).astype(F32), ((0, 0), (0, LANES - v.shape[-1])), constant_values=fill)


def _layer(h, mem, p, merge_tm=512):
    bsz, seq, d = h.shape
    assert d == SUBLANES * LANES, "the row-tile layout needs one (8,128) tile per model row"
    t = bsz * seq
    mlen = mem.shape[1]
    inner = SSD_HEADS * SSD_HEAD_DIM
    conv_ch = inner + 2 * SSD_GROUPS * SSD_STATE
    sbi = SB_HEADS * SB_HEAD_DIM
    n_experts = p["w_router"].shape[1]
    x2 = h.reshape(t, d)

    w_in = p["w_in"]
    dt0 = inner + conv_ch
    w_main = jnp.concatenate([w_in[:, :dt0], w_in[:, dt0 + SSD_HEADS:]], axis=1).astype(BF16)
    w_dt = jnp.pad(w_in[:, dt0:dt0 + SSD_HEADS], ((0, 0), (0, LANES - SSD_HEADS))).astype(BF16)
    proj, dt_raw = inproj(x2, p["g_mix"].reshape(1, d), w_main, w_dt)
    qcol = dt0 // LANES
    kcol = qcol + sbi // LANES
    vcol = kcol + sbi // LANES
    gcol = (dt0 + 3 * sbi) // d

    heads = jnp.arange(LANES)[:, None]
    chan = jnp.arange(inner)[None, :]
    e_mat = (chan // SSD_HEAD_DIM == heads).astype(BF16)
    y_ssd = ssd(proj, dt_raw, p["conv_w"], p["conv_b"].reshape(1, conv_ch), _pad_lanes(p["dt_bias"]),
                _pad_lanes(p["a_log"]), jnp.repeat(p["d_skip"], SSD_HEAD_DIM).reshape(1, inner),
                p["g_ssd"].reshape(1, inner), e_mat, bsz, seq)

    jj = jnp.arange(SB_T)[:, None]
    ss = jnp.arange(SB_T + LANES)[None, :]
    u_mat = ((ss >= SB_T) | (jj > ss)).astype(BF16)
    y_sb = sb_attn(proj, u_mat, bsz, seq, qcol, kcol, vcol)

    kv = mem_kv(mem.reshape(bsz * mlen, d), p["g_mem"].reshape(1, d), p["w_xkv"].astype(BF16), mlen)

    wr = jnp.pad(p["w_router"], ((0, 0), (0, LANES - n_experts)))
    wrh = wr.astype(BF16)
    wrl = (wr - wrh.astype(F32)).astype(BF16)
    tri = (jnp.arange(merge_tm)[None, :] <= jnp.arange(merge_tm)[:, None]).astype(BF16)
    h2, hn, eidx, rank, gate, counts = merge(
        x2, y_ssd, y_sb, proj, gcol, p["w_ssd_out"].astype(BF16), p["w_sb_out"].astype(BF16),
        p["w_out"].astype(BF16), p["g_xattn"].reshape(1, d), p["w_xq"].astype(BF16), kv,
        p["w_xo"].astype(BF16), p["g_moe"].reshape(1, d), wrh, wrl,
        _pad_lanes(p["b_router"], NEG_BIG), tri, seq, mlen, merge_tm)

    cnt = counts[0, :n_experts]
    padded = (cnt + MOE_BLOCK - 1) // MOE_BLOCK * MOE_BLOCK
    pend = jnp.cumsum(padded)
    pstart = (pend - padded).astype(I32)
    n_blocks = -(-(t * TOP_K) // MOE_BLOCK) + n_experts
    n_rows = n_blocks * MOE_BLOCK
    n_used = (pend[-1] // MOE_BLOCK).astype(I32)
    blk = jnp.minimum(jnp.arange(n_blocks, dtype=I32), n_used - 1) * MOE_BLOCK
    blk_expert = jnp.sum(blk[:, None] >= pend[None, :n_experts - 1], axis=1).astype(I32)
    e_flat = eidx[:, :TOP_K].reshape(-1)
    r_flat = rank[:, :TOP_K].reshape(-1)
    pad_lo = jnp.concatenate([pstart + cnt, pend[-1:]]).astype(I32)
    pad_hi = jnp.concatenate([pend, jnp.full((1,), n_rows)]).astype(I32)

    xin = dispatch(pstart, pad_lo, pad_hi, e_flat, r_flat, hn, n_rows)
    yb = moe_ffn(blk_expert, n_used.reshape(1), xin, p["w_gu"], p["b_gu"][:, None, :],
                 p["w_down"], p["b_down"][:, None, :])
    return h2, (pstart, e_flat, r_flat, yb, gate)


def kernel(x, mem, g_mix, w_in, conv_w, conv_b, dt_bias, a_log, d_skip, g_ssd, w_ssd_out, w_sb_out,
           w_out, g_xattn, g_mem, w_xq, w_xkv, w_xo, g_moe, w_router, b_router, w_gu, b_gu, w_down,
           b_down, g_final):
    stacked = dict(g_mix=g_mix, w_in=w_in, conv_w=conv_w, conv_b=conv_b, dt_bias=dt_bias, a_log=a_log,
                   d_skip=d_skip, g_ssd=g_ssd, w_ssd_out=w_ssd_out, w_sb_out=w_sb_out, w_out=w_out,
                   g_xattn=g_xattn, g_mem=g_mem, w_xq=w_xq, w_xkv=w_xkv, w_xo=w_xo, g_moe=g_moe,
                   w_router=w_router, b_router=b_router, w_gu=w_gu, b_gu=b_gu, w_down=w_down,
                   b_down=b_down)
    assert g_mix.shape[0] == 1, "stacked depth other than 1 is not supported"
    bsz, seq, d = x.shape
    p = {k: v[0] for k, v in stacked.items()}
    h2, (pstart, e_flat, r_flat, yb, gate) = _layer(x, mem, p)
    out = combine(pstart, e_flat, r_flat, yb, h2, gate, g_final.reshape(1, d))
    return out.reshape(bsz, seq, d)
```

```python
import functools

import jax
import jax.numpy as jnp
from jax import lax
from jax.experimental import pallas as pl
from jax.experimental.pallas import tpu as pltpu

F32 = jnp.float32
BF16 = jnp.bfloat16
I32 = jnp.int32

LANES = 128
SUBLANES = 8
EPS = 1e-6
VMEM_LIMIT = 56 * 1024 * 1024

SSD_HEADS = 16
SSD_HEAD_DIM = 64
SSD_GROUPS = 2
SSD_STATE = 128
SSD_CONV = 4
SB_HEADS = 8
SB_HEAD_DIM = 64
XA_HEADS = 4
XA_HEAD_DIM = 128
TOP_K = 4
SWIGLU_LIMIT = 7.0
SWIGLU_ALPHA = 1.702

INPROJ_TN = 1024
SSD_L = 128
SB_T = 256
SB_SKIP = 110.0
MOE_BLOCK = 512
MOE_CAST_ROWS = 128
ZERO_ROWS_LOG2 = 5
ZERO_ROWS = 1 << ZERO_ROWS_LOG2
ROW_DMA_UNROLL = 2
NEG_BIG = -1e30


def _dot(a, b):
    return jnp.dot(a, b, preferred_element_type=F32)


def _dot_nt(a, b):
    return lax.dot_general(a, b, (((1,), (1,)), ((), ())), preferred_element_type=F32)


def _split_bf16(v):
    hi = v.astype(BF16)
    lo = (v - hi.astype(F32)).astype(BF16)
    return hi, lo


def _dot2(v, m):
    hi, lo = _split_bf16(v)
    return _dot(hi, m) + _dot(lo, m)


def _softplus(x):
    return jnp.maximum(x, 0.0) + jnp.log(1.0 + jnp.exp(-jnp.abs(x)))


def _sigmoid(x):
    return 1.0 / (1.0 + jnp.exp(-x))


def _params(sem, vmem=VMEM_LIMIT):
    return pltpu.CompilerParams(dimension_semantics=sem, vmem_limit_bytes=vmem)


def _store_rows(ref, val):
    n = val.shape[0]
    for s in range(SUBLANES):
        ref[pl.ds(s, n, stride=SUBLANES), :] = val[:, s * LANES:(s + 1) * LANES]


def _load_rows(ref, n):
    return jnp.concatenate([ref[pl.ds(s, n, stride=SUBLANES), :] for s in range(SUBLANES)], axis=1)


def _row(ref, r):
    return ref.at[pl.ds(pl.multiple_of(r * SUBLANES, SUBLANES), SUBLANES), :]


def _inproj_kernel(x_ref, g_ref, w_ref, wdt_ref, proj_ref, dt_ref):
    x = x_ref[...]
    n = x * lax.rsqrt(jnp.mean(x * x, axis=-1, keepdims=True) + EPS) * g_ref[...]
    nb = n.astype(BF16)
    dt_ref[...] = _dot(nb, wdt_ref[...])
    for c in range(0, w_ref.shape[1], INPROJ_TN):
        proj_ref[:, c:c + INPROJ_TN] = _dot(nb, w_ref[:, c:c + INPROJ_TN]).astype(BF16)


def inproj(x2, g, w, wdt, tm=512):
    t, d = x2.shape
    n = w.shape[1]
    return pl.pallas_call(
        _inproj_kernel,
        grid=(t // tm,),
        in_specs=[
            pl.BlockSpec((tm, d), lambda i: (i, 0)),
            pl.BlockSpec((1, d), lambda i: (0, 0)),
            pl.BlockSpec((d, n), lambda i: (0, 0), pipeline_mode=pl.Buffered(1)),
            pl.BlockSpec((d, LANES), lambda i: (0, 0)),
        ],
        out_specs=[
            pl.BlockSpec((tm, n), lambda i: (i, 0)),
            pl.BlockSpec((tm, LANES), lambda i: (i, 0)),
        ],
        out_shape=[
            jax.ShapeDtypeStruct((t, n), BF16),
            jax.ShapeDtypeStruct((t, LANES), F32),
        ],
        compiler_params=_params(("arbitrary",)),
        name="inproj",
    )(x2, g, w, wdt)


def _ssd_kernel(z_ref, xs_ref, bc_ref, dt_ref, cw_ref, cb_ref, dtb_ref, alog_ref, dskip_ref,
                gn_ref, e_ref, o_ref, ext_scr, state_scr):
    L = SSD_L
    inner = SSD_HEADS * SSD_HEAD_DIM
    gw = inner // SSD_GROUPS
    n = SSD_STATE

    @pl.when(pl.program_id(1) == 0)
    def _():
        ext_scr[0:SUBLANES, :] = jnp.zeros((SUBLANES, ext_scr.shape[1]), F32)
        state_scr[...] = jnp.zeros(state_scr.shape, F32)

    ext_scr[SUBLANES:SUBLANES + L, 0:inner] = xs_ref[...].astype(F32)
    ext_scr[SUBLANES:SUBLANES + L, inner:] = bc_ref[...].astype(F32)
    conv = cb_ref[...]
    for k in range(SSD_CONV):
        off = SUBLANES - (SSD_CONV - 1) + k
        conv = conv + cw_ref[k:k + 1, :] * ext_scr[off:off + L, :]
    ext_scr[0:SUBLANES, :] = ext_scr[L:L + SUBLANES, :]
    xbc = conv * _sigmoid(conv)
    xs = xbc[:, :inner]

    dt = _softplus(dt_ref[...] + dtb_ref[...])
    da = dt * (-jnp.exp(alog_ref[...]))
    row = lax.broadcasted_iota(I32, (L, L), 0)
    col = lax.broadcasted_iota(I32, (L, L), 1)
    tril = col <= row
    cs = jnp.dot(tril.astype(F32), da, precision=lax.Precision.HIGHEST,
                 preferred_element_type=F32)
    cst = cs.T
    cs_last = cs[L - 1:L, :]

    e = e_ref[...]
    xdt = xs * _dot2(dt, e)
    xw = (xdt * _dot2(jnp.exp(cs_last - cs), e)).astype(BF16)
    od_e = _dot2(jnp.exp(cs), e)
    xdt_b = xdt.astype(BF16)
    lane = lax.broadcasted_iota(I32, (L, LANES), 1)

    y_parts = []
    new_states = []
    for g in range(SSD_GROUPS):
        bg = xbc[:, inner + g * n:inner + (g + 1) * n]
        cg = xbc[:, inner + SSD_GROUPS * n + g * n:inner + SSD_GROUPS * n + (g + 1) * n]
        bg_b = bg.astype(BF16)
        cg_b = cg.astype(BF16)
        cbm = _dot_nt(cg_b, bg_b)
        st = state_scr[:, g * gw:(g + 1) * gw]
        y_off = _dot(cg_b, st.astype(BF16))
        new_states.append(_dot(bg.T.astype(BF16), xw[:, g * gw:(g + 1) * gw]))
        heads_per_group = SSD_HEADS // SSD_GROUPS
        for hp in range(heads_per_group // 2):
            pair = xdt_b[:, g * gw + hp * LANES:g * gw + (hp + 1) * LANES]
            yd = []
            for sub in range(2):
                h = g * heads_per_group + hp * 2 + sub
                seg = cs[:, h:h + 1] - cst[h:h + 1, :]
                lm = jnp.exp(jnp.where(tril, seg, -jnp.inf))
                yd.append(_dot((cbm * lm).astype(BF16), pair))
            y_parts.append((jnp.where(lane < SSD_HEAD_DIM, yd[0], yd[1]), g, hp, y_off))

    cols = []
    for (yd, g, hp, y_off) in y_parts:
        c0 = g * gw + hp * LANES
        cols.append(yd + y_off[:, hp * LANES:(hp + 1) * LANES] * od_e[:, c0:c0 + LANES])
    y = jnp.concatenate(cols, axis=1) + xs * dskip_ref[...]
    zf = z_ref[...].astype(F32)
    y = y * (zf * _sigmoid(zf))
    outs = []
    for g in range(SSD_GROUPS):
        yg = y[:, g * gw:(g + 1) * gw]
        ms = jnp.mean(yg * yg, axis=-1, keepdims=True)
        outs.append(yg * lax.rsqrt(ms + EPS) * gn_ref[:, g * gw:(g + 1) * gw])
    o_ref[...] = jnp.concatenate(outs, axis=1).astype(BF16)

    cd_e = od_e[L - 1:L, :]
    for g in range(SSD_GROUPS):
        state_scr[:, g * gw:(g + 1) * gw] = (
            state_scr[:, g * gw:(g + 1) * gw] * cd_e[:, g * gw:(g + 1) * gw] + new_states[g])


def ssd(proj, dt_raw, conv_w, conv_b, dtb, alog, dskip_e, gn, e_mat, bsz, seq):
    L = SSD_L
    inner = SSD_HEADS * SSD_HEAD_DIM
    bcw = 2 * SSD_GROUPS * SSD_STATE
    nc = seq // L
    ch = inner + bcw
    full = lambda shape: pl.BlockSpec(shape, lambda b, c: (0,) * len(shape))
    return pl.pallas_call(
        _ssd_kernel,
        grid=(bsz, nc),
        in_specs=[
            pl.BlockSpec((L, inner), lambda b, c: (b * nc + c, 0)),
            pl.BlockSpec((L, inner), lambda b, c: (b * nc + c, 1)),
            pl.BlockSpec((L, bcw), lambda b, c: (b * nc + c, 2 * inner // bcw)),
            pl.BlockSpec((L, LANES), lambda b, c: (b * nc + c, 0)),
            full((SSD_CONV, ch)), full((1, ch)), full((1, LANES)), full((1, LANES)),
            full((1, inner)), full((1, inner)), full((LANES, inner)),
        ],
        out_specs=pl.BlockSpec((L, inner), lambda b, c: (b * nc + c, 0)),
        out_shape=jax.ShapeDtypeStruct((bsz * seq, inner), BF16),
        scratch_shapes=[pltpu.VMEM((L + 2 * SUBLANES, ch), F32), pltpu.VMEM((SSD_STATE, inner), F32)],
        compiler_params=_params(("arbitrary", "arbitrary")),
        name="ssd",
    )(proj, proj, proj, dt_raw, conv_w, conv_b, dtb, alog, dskip_e, gn, e_mat)


def _sb_kernel(q_ref, k_ref, v_ref, u_ref, o_ref, carry_scr, acc_scr):
    t = SB_T
    nh = LANES // SB_HEAD_DIM
    i = pl.program_id(2)
    lane = lax.broadcasted_iota(I32, (t, LANES), 1)
    q = q_ref[...]
    scale = SB_HEAD_DIM ** -0.5
    qh = []
    for h in range(nh):
        in_head = (lane >= h * SB_HEAD_DIM) & (lane < (h + 1) * SB_HEAD_DIM)
        qh.append((jnp.where(in_head, q, jnp.zeros_like(q)).astype(F32) * scale).astype(BF16))
    u = u_ref[...]

    def step(j, diagonal, state):
        ks = pl.multiple_of(j * t, t)
        kb = k_ref[pl.ds(ks, t), :]
        vb = v_ref[pl.ds(ks, t), :]
        if diagonal:
            mask = lax.broadcasted_iota(I32, (t, t), 1) < lax.broadcasted_iota(I32, (t, t), 0)
        out = []
        for h in range(nh):
            x = _dot_nt(qh[h], kb)
            sp = _softplus(x)
            if diagonal:
                sp = jnp.where(mask, sp, 0.0)
            suf = _dot(sp.astype(BF16), u)
            e = x - sp - suf[:, :t]
            if state is not None:
                e = e - jnp.concatenate([state[h][0]] * (t // LANES), axis=1)
            w = jnp.exp(e)
            if diagonal:
                w = jnp.where(mask, w, 0.0)
            pv = _dot(w.astype(BF16), vb)
            if state is None:
                out.append((suf[:, t:], pv))
            else:
                out.append((state[h][0] + suf[:, t:], state[h][1] + pv))
        return out

    def smallest(state):
        return functools.reduce(jnp.minimum, [jnp.min(c) for c, _ in state])

    def save(state):
        for h in range(nh):
            carry_scr[h] = state[h][0]
            acc_scr[h] = state[h][1]

    def cond(st):
        j, go = st
        return (j >= 0) & (go > 0)

    def body(st):
        j, _ = st
        state = step(j, False, [(carry_scr[h], acc_scr[h]) for h in range(nh)])
        save(state)
        return j - 1, (smallest(state) < SB_SKIP).astype(I32)

    @pl.when(i == 0)
    def _():
        save(step(0, True, None))

    @pl.when(i > 0)
    def _():
        state = step(i - 1, False, step(i, True, None))
        save(state)
        lax.while_loop(cond, body, (i - 2, (smallest(state) < SB_SKIP).astype(I32)))

    o_ref[...] = jnp.where(lane < SB_HEAD_DIM, acc_scr[0], acc_scr[1]).astype(BF16)


def sb_attn(proj, u_mat, bsz, seq, qcol, kcol, vcol):
    t = SB_T
    nq = seq // t
    npairs = SB_HEADS * SB_HEAD_DIM // LANES
    nh = LANES // SB_HEAD_DIM
    return pl.pallas_call(
        _sb_kernel,
        grid=(bsz, npairs, nq),
        in_specs=[
            pl.BlockSpec((t, LANES), lambda b, p, i: (b * nq + i, qcol + p)),
            pl.BlockSpec((seq, LANES), lambda b, p, i: (b, kcol + p)),
            pl.BlockSpec((seq, LANES), lambda b, p, i: (b, vcol + p)),
            pl.BlockSpec((t, t + LANES), lambda b, p, i: (0, 0)),
        ],
        out_specs=pl.BlockSpec((t, LANES), lambda b, p, i: (b * nq + i, p)),
        out_shape=jax.ShapeDtypeStruct((bsz * seq, npairs * LANES), BF16),
        scratch_shapes=[pltpu.VMEM((nh, t, LANES), F32), pltpu.VMEM((nh, t, LANES), F32)],
        compiler_params=_params(("arbitrary", "arbitrary", "arbitrary")),
        name="sb_attn",
    )(proj, proj, proj, u_mat)


def _memkv_kernel(m_ref, g_ref, w_ref, o_ref):
    x = m_ref[...]
    n = x * lax.rsqrt(jnp.mean(x * x, axis=-1, keepdims=True) + EPS) * g_ref[...]
    o_ref[...] = _dot(n.astype(BF16), w_ref[...]).astype(BF16)


def mem_kv(mem2, g, w, mlen):
    rows, d = mem2.shape
    n = w.shape[1]
    return pl.pallas_call(
        _memkv_kernel,
        grid=(rows // mlen,),
        in_specs=[pl.BlockSpec((mlen, d), lambda b: (b, 0)),
                  pl.BlockSpec((1, d), lambda b: (0, 0)),
                  pl.BlockSpec((d, n), lambda b: (0, 0))],
        out_specs=pl.BlockSpec((mlen, n), lambda b: (b, 0)),
        out_shape=jax.ShapeDtypeStruct((rows, n), BF16),
        compiler_params=_params(("arbitrary",)),
        name="mem_kv",
    )(mem2, g, w)


def _merge_kernel(x_ref, yssd_ref, ysb_ref, gssd_ref, gsb_ref, wso_ref, wsbo_ref, wo_ref, gx_ref,
                  wxq_ref, kv_ref, wxo_ref, gm_ref, wrh_ref, wrl_ref, br_ref, tri_ref,
                  h2_ref, hn_ref, eidx_ref, rank_ref, gate_ref, cnt_ref, cnt_scr):
    @pl.when(pl.program_id(0) == 0)
    def _():
        cnt_scr[...] = jnp.zeros(cnt_scr.shape, F32)

    tm = x_ref.shape[0]
    y_ssd = _dot(yssd_ref[...], wso_ref[...])
    y_sb = _dot(ysb_ref[...], wsbo_ref[...])
    merged = (_sigmoid(gssd_ref[...].astype(F32)) * y_ssd
              + _sigmoid(gsb_ref[...].astype(F32)) * y_sb)
    h1 = x_ref[...] + _dot(merged.astype(BF16), wo_ref[...])

    n2 = h1 * lax.rsqrt(jnp.mean(h1 * h1, axis=-1, keepdims=True) + EPS) * gx_ref[...]
    q = (_dot(n2.astype(BF16), wxq_ref[...]) * (XA_HEAD_DIM ** -0.5)).astype(BF16)
    xi = XA_HEADS * XA_HEAD_DIM
    heads = []
    for h in range(XA_HEADS):
        kh = kv_ref[:, h * XA_HEAD_DIM:(h + 1) * XA_HEAD_DIM]
        vh = kv_ref[:, xi + h * XA_HEAD_DIM:xi + (h + 1) * XA_HEAD_DIM]
        lg = _dot_nt(q[:, h * XA_HEAD_DIM:(h + 1) * XA_HEAD_DIM], kh)
        ex = jnp.exp(lg - jnp.max(lg, axis=-1, keepdims=True))
        den = jnp.sum(ex, axis=-1, keepdims=True)
        heads.append(_dot(ex.astype(BF16), vh) / den)
    o = jnp.concatenate(heads, axis=1).astype(BF16)
    h2 = h1 + _dot(o, wxo_ref[...])
    h2_ref[...] = h2

    n3 = h2 * lax.rsqrt(jnp.mean(h2 * h2, axis=-1, keepdims=True) + EPS) * gm_ref[...]
    _store_rows(hn_ref, n3)

    nh, nl = _split_bf16(n3)
    lg = _dot(nh, wrh_ref[...]) + _dot(nl, wrh_ref[...]) + _dot(nh, wrl_ref[...]) + br_ref[...]
    lane = lax.broadcasted_iota(I32, (tm, LANES), 1)
    lanef = lane.astype(F32)
    work = lg
    vals, idxs = [], []
    for _ in range(TOP_K):
        m = jnp.max(work, axis=-1, keepdims=True)
        idx = jnp.min(jnp.where(work == m, lanef, float(LANES)), axis=-1, keepdims=True)
        vals.append(m)
        idxs.append(idx)
        work = jnp.where(lanef == idx, NEG_BIG * 2.0, work)
    exps = [jnp.exp(v - vals[0]) for v in vals]
    den = exps[0]
    for ex in exps[1:]:
        den = den + ex

    sel = jnp.zeros((tm, LANES), F32)
    for idx in idxs:
        sel = sel + (lanef == idx).astype(F32)
    incl = _dot(tri_ref[...], sel.astype(BF16))
    before = incl - sel + cnt_scr[...]
    eidx = jnp.zeros((tm, LANES), F32)
    rank = jnp.zeros((tm, LANES), F32)
    gate = jnp.zeros((tm, LANES), F32)
    for k in range(TOP_K):
        rk = jnp.sum(jnp.where(lanef == idxs[k], before, 0.0), axis=-1, keepdims=True)
        eidx = jnp.where(lane == k, idxs[k], eidx)
        rank = jnp.where(lane == k, rk, rank)
        gate = jnp.where(lane == k, exps[k] / den, gate)
    eidx_ref[...] = eidx.astype(I32)
    rank_ref[...] = rank.astype(I32)
    gate_ref[...] = gate
    cnt_scr[...] = cnt_scr[...] + incl[tm - 1:tm, :]
    cnt_ref[...] = cnt_scr[...].astype(I32)


def merge(x2, y_ssd, y_sb, proj, gcol, wso, wsbo, wo, gx, wxq, kv, wxo, gm, wrh, wrl, br, tri,
          seq, mlen, tm):
    t, d = x2.shape
    si = y_ssd.shape[1]
    sbi = y_sb.shape[1]
    xi2 = kv.shape[1]
    full = lambda a: pl.BlockSpec(a.shape, lambda i: (0,) * a.ndim)
    row = lambda w: pl.BlockSpec((tm, w), lambda i: (i, 0))
    return pl.pallas_call(
        _merge_kernel,
        grid=(t // tm,),
        in_specs=[
            row(d), row(si), row(sbi),
            pl.BlockSpec((tm, d), lambda i: (i, gcol)),
            pl.BlockSpec((tm, d), lambda i: (i, gcol + 1)),
            full(wso), full(wsbo), full(wo), full(gx), full(wxq),
            pl.BlockSpec((mlen, xi2), lambda i: ((i * tm) // seq, 0)),
            full(wxo), full(gm), full(wrh), full(wrl), full(br), full(tri),
        ],
        out_specs=[
            row(d),
            pl.BlockSpec((tm * d // LANES, LANES), lambda i: (i, 0)),
            row(LANES), row(LANES), row(LANES),
            pl.BlockSpec((1, LANES), lambda i: (0, 0)),
        ],
        out_shape=[
            jax.ShapeDtypeStruct((t, d), F32),
            jax.ShapeDtypeStruct((t * d // LANES, LANES), F32),
            jax.ShapeDtypeStruct((t, LANES), I32),
            jax.ShapeDtypeStruct((t, LANES), I32),
            jax.ShapeDtypeStruct((t, LANES), F32),
            jax.ShapeDtypeStruct((1, LANES), I32),
        ],
        scratch_shapes=[pltpu.VMEM((1, LANES), F32)],
        compiler_params=_params(("arbitrary",)),
        name="merge",
    )(x2, y_ssd, y_sb, proj, proj, wso, wsbo, wo, gx, wxq, kv, wxo, gm, wrh, wrl, br, tri)


def _index_copies(e_hbm, r_hbm, e_sm, r_sm, isem, step, slot):
    n = e_sm.shape[0] // 2
    return (pltpu.make_async_copy(e_hbm.at[pl.ds(step * n, n)], e_sm.at[pl.ds(slot * n, n)], isem.at[0, slot]),
            pltpu.make_async_copy(r_hbm.at[pl.ds(step * n, n)], r_sm.at[pl.ds(slot * n, n)], isem.at[1, slot]))


def _issue_rows(ps_ref, e_sm, r_sm, slot, tm, copy_for):
    n = tm * TOP_K

    def body(it, c):
        toks = [it * ROW_DMA_UNROLL + u for u in range(ROW_DMA_UNROLL)]
        rows = []
        for tok in toks:
            for k in range(TOP_K):
                a = slot * n + tok * TOP_K + k
                rows.append(ps_ref[e_sm[a]] + r_sm[a])
        for u, tok in enumerate(toks):
            for k in range(TOP_K):
                copy_for(tok, k, rows[u * TOP_K + k]).start(priority=k % 2)
        return c

    lax.fori_loop(0, tm // ROW_DMA_UNROLL, body, 0)


def _dispatch_kernel(ps_ref, lo_ref, hi_ref, e_hbm, r_hbm, hn_ref, xin_out, e_sm, r_sm, zero_scr,
                     isem, sem, zsem):
    n = e_sm.shape[0] // 2
    tm = n // TOP_K
    i = pl.program_id(0)
    last = pl.num_programs(0) - 1
    slot = i % 2
    idx = functools.partial(_index_copies, e_hbm, r_hbm, e_sm, r_sm, isem)

    @pl.when(i == 0)
    def _():
        for c in idx(0, 0):
            c.start()

    for c in idx(i, slot):
        c.wait()

    @pl.when(i < last)
    def _():
        for c in idx(i + 1, 1 - slot):
            c.start()

    _issue_rows(ps_ref, e_sm, r_sm, slot, tm,
                lambda tok, k, row: pltpu.make_async_copy(_row(hn_ref, tok), _row(xin_out, row), sem))

    @pl.when(i == last)
    def _():
        zero_scr[...] = jnp.zeros(zero_scr.shape, F32)

        def fill(start, count, rows):
            def copy(j):
                dst = xin_out.at[pl.ds(pl.multiple_of((start + j * rows) * SUBLANES, SUBLANES), rows * SUBLANES), :]
                return pltpu.make_async_copy(zero_scr.at[pl.ds(0, rows * SUBLANES), :], dst, zsem)

            def zissue(j, c2):
                copy(j).start()
                return c2

            def zdrain(j, c2):
                copy(j).wait()
                return c2

            lax.fori_loop(0, count, zissue, 0)
            lax.fori_loop(0, count, zdrain, 0)

        def per_span(s, c):
            lo = lo_ref[s]
            length = hi_ref[s] - lo
            big = lax.shift_right_logical(length, ZERO_ROWS_LOG2)
            fill(lo, big, ZERO_ROWS)
            fill(lo + big * ZERO_ROWS, length - big * ZERO_ROWS, 1)
            return c

        lax.fori_loop(0, lo_ref.shape[0], per_span, 0)

    for k in range(TOP_K):
        pltpu.make_async_copy(hn_ref, xin_out.at[pl.ds(0, tm * SUBLANES), :], sem).wait()


def dispatch(pstart, pad_lo, pad_hi, e_flat, r_flat, hn, n_rows, tm=1024):
    t = hn.shape[0] // SUBLANES
    grid_spec = pltpu.PrefetchScalarGridSpec(
        num_scalar_prefetch=3,
        grid=(t // tm,),
        in_specs=[pl.BlockSpec(memory_space=pl.ANY), pl.BlockSpec(memory_space=pl.ANY),
                  pl.BlockSpec((tm * SUBLANES, LANES), lambda i, ps, lo, hi: (i, 0))],
        out_specs=pl.BlockSpec(memory_space=pl.ANY),
        scratch_shapes=[pltpu.SMEM((2 * tm * TOP_K,), I32), pltpu.SMEM((2 * tm * TOP_K,), I32),
                        pltpu.VMEM((ZERO_ROWS * SUBLANES, LANES), F32),
                        pltpu.SemaphoreType.DMA((2, 2)), pltpu.SemaphoreType.DMA,
                        pltpu.SemaphoreType.DMA],
    )
    return pl.pallas_call(
        _dispatch_kernel,
        grid_spec=grid_spec,
        out_shape=jax.ShapeDtypeStruct((n_rows * SUBLANES, LANES), F32),
        compiler_params=_params(("arbitrary",)),
        name="dispatch",
    )(pstart, pad_lo, pad_hi, e_flat, r_flat, hn)


def _moe_kernel(be_ref, nu_ref, x_ref, wgu_ref, bgu_ref, wd_ref, bd_ref, o_ref, wgu_scr, wd_scr):
    i = pl.program_id(0)
    dff = wd_ref.shape[1]
    active = i < nu_ref[0]

    @pl.when(active & ((i == 0) | (be_ref[i] != be_ref[jnp.maximum(i - 1, 0)])))
    def _():
        for r in range(0, wgu_ref.shape[1], MOE_CAST_ROWS):
            wgu_scr[r:r + MOE_CAST_ROWS, :] = wgu_ref[0, r:r + MOE_CAST_ROWS, :].astype(BF16)
        for r in range(0, dff, MOE_CAST_ROWS):
            wd_scr[r:r + MOE_CAST_ROWS, :] = wd_ref[0, r:r + MOE_CAST_ROWS, :].astype(BF16)

    @pl.when(active)
    def _():
        x = _load_rows(x_ref, MOE_BLOCK).astype(BF16)
        gu = _dot(x, wgu_scr[...]) + bgu_ref[0]
        g_lin = jnp.minimum(gu[:, :dff], SWIGLU_LIMIT)
        u_lin = jnp.clip(gu[:, dff:], -SWIGLU_LIMIT, SWIGLU_LIMIT)
        act = g_lin * _sigmoid(SWIGLU_ALPHA * g_lin) * (u_lin + 1.0)
        _store_rows(o_ref, _dot(act.astype(BF16), wd_scr[...]) + bd_ref[0])

    @pl.when(jnp.logical_not(active))
    def _():
        o_ref[...] = jnp.zeros(o_ref.shape, F32)


def moe_ffn(blk_expert, n_used, xin, wgu, bgu, wd, bd):
    blk_rows = MOE_BLOCK * SUBLANES
    nb = xin.shape[0] // blk_rows
    d = wgu.shape[1]
    f2 = wgu.shape[2]
    dff = wd.shape[1]
    grid_spec = pltpu.PrefetchScalarGridSpec(
        num_scalar_prefetch=2,
        grid=(nb,),
        in_specs=[
            pl.BlockSpec((blk_rows, LANES), lambda i, be, nu: (jnp.minimum(i, nu[0] - 1), 0)),
            pl.BlockSpec((1, d, f2), lambda i, be, nu: (be[i], 0, 0)),
            pl.BlockSpec((1, 1, f2), lambda i, be, nu: (be[i], 0, 0)),
            pl.BlockSpec((1, dff, d), lambda i, be, nu: (be[i], 0, 0)),
            pl.BlockSpec((1, 1, d), lambda i, be, nu: (be[i], 0, 0)),
        ],
        out_specs=pl.BlockSpec((blk_rows, LANES), lambda i, be, nu: (i, 0)),
        scratch_shapes=[pltpu.VMEM((d, f2), BF16), pltpu.VMEM((dff, d), BF16)],
    )
    return pl.pallas_call(
        _moe_kernel,
        grid_spec=grid_spec,
        out_shape=jax.ShapeDtypeStruct(xin.shape, F32),
        compiler_params=_params(("arbitrary",)),
        name="moe_ffn",
    )(blk_expert, n_used, xin, wgu, bgu, wd, bd)


def _combine_kernel(ps_ref, e_hbm, r_hbm, yb_hbm, h2_ref, gate_ref, gf_ref, o_ref,
                    e_sm, r_sm, buf, isem, sem):
    tm = h2_ref.shape[0]
    i = pl.program_id(0)
    last = pl.num_programs(0) - 1
    slot = i % 2
    idx = functools.partial(_index_copies, e_hbm, r_hbm, e_sm, r_sm, isem)

    def gather(s):
        _issue_rows(ps_ref, e_sm, r_sm, s, tm,
                    lambda tok, k, row: pltpu.make_async_copy(_row(yb_hbm, row), _row(buf.at[s, k], tok),
                                                              sem.at[s]))

    @pl.when(i == 0)
    def _():
        for c in idx(0, 0):
            c.start()
        for c in idx(0, 0):
            c.wait()
        gather(0)

        @pl.when(last > 0)
        def _():
            for c in idx(1, 1):
                c.start()

    @pl.when(i < last)
    def _():
        for c in idx(i + 1, 1 - slot):
            c.wait()
        gather(1 - slot)

    @pl.when(i + 2 <= last)
    def _():
        for c in idx(i + 2, slot):
            c.start()

    for k in range(TOP_K):
        pltpu.make_async_copy(yb_hbm.at[pl.ds(0, tm * SUBLANES), :], buf.at[slot, k], sem.at[slot]).wait()

    gate = gate_ref[...]
    h3 = h2_ref[...]
    for k in range(TOP_K):
        h3 = h3 + gate[:, k:k + 1] * _load_rows(buf.at[slot, k], tm)
    o_ref[...] = h3 * lax.rsqrt(jnp.mean(h3 * h3, axis=-1, keepdims=True) + EPS) * gf_ref[...]


def combine(pstart, e_flat, r_flat, yb, h2, gate, gf, tm=512):
    t, d = h2.shape
    grid_spec = pltpu.PrefetchScalarGridSpec(
        num_scalar_prefetch=1,
        grid=(t // tm,),
        in_specs=[
            pl.BlockSpec(memory_space=pl.ANY),
            pl.BlockSpec(memory_space=pl.ANY),
            pl.BlockSpec(memory_space=pl.ANY),
            pl.BlockSpec((tm, d), lambda i, ps: (i, 0)),
            pl.BlockSpec((tm, LANES), lambda i, ps: (i, 0)),
            pl.BlockSpec((1, d), lambda i, ps: (0, 0)),
        ],
        out_specs=pl.BlockSpec((tm, d), lambda i, ps: (i, 0)),
        scratch_shapes=[pltpu.SMEM((2 * tm * TOP_K,), I32), pltpu.SMEM((2 * tm * TOP_K,), I32),
                        pltpu.VMEM((2, TOP_K, tm * SUBLANES, LANES), F32),
                        pltpu.SemaphoreType.DMA((2, 2)), pltpu.SemaphoreType.DMA((2,))],
    )
    return pl.pallas_call(
        _combine_kernel,
        grid_spec=grid_spec,
        out_shape=jax.ShapeDtypeStruct((t, d), F32),
        compiler_params=_params(("arbitrary",)),
        name="combine",
    )(pstart, e_flat, r_flat, yb, h2, gate, gf)


def _pad_lanes(v, fill=0.0):
    return jnp.pad(v.reshape(1, -1).astype(F32), ((0, 0), (0, LANES - v.shape[-1])), constant_values=fill)


def _layer(h, mem, p, merge_tm=512):
    bsz, seq, d = h.shape
    assert d == SUBLANES * LANES, "the row-tile layout needs one (8,128) tile per model row"
    t = bsz * seq
    mlen = mem.shape[1]
    inner = SSD_HEADS * SSD_HEAD_DIM
    conv_ch = inner + 2 * SSD_GROUPS * SSD_STATE
    sbi = SB_HEADS * SB_HEAD_DIM
    n_experts = p["w_router"].shape[1]
    x2 = h.reshape(t, d)

    w_in = p["w_in"]
    dt0 = inner + conv_ch
    w_main = jnp.concatenate([w_in[:, :dt0], w_in[:, dt0 + SSD_HEADS:]], axis=1).astype(BF16)
    w_dt = jnp.pad(w_in[:, dt0:dt0 + SSD_HEADS], ((0, 0), (0, LANES - SSD_HEADS))).astype(BF16)
    proj, dt_raw = inproj(x2, p["g_mix"].reshape(1, d), w_main, w_dt)
    qcol = dt0 // LANES
    kcol = qcol + sbi // LANES
    vcol = kcol + sbi // LANES
    gcol = (dt0 + 3 * sbi) // d

    heads = jnp.arange(LANES)[:, None]
    chan = jnp.arange(inner)[None, :]
    e_mat = (chan // SSD_HEAD_DIM == heads).astype(BF16)
    y_ssd = ssd(proj, dt_raw, p["conv_w"], p["conv_b"].reshape(1, conv_ch), _pad_lanes(p["dt_bias"]),
                _pad_lanes(p["a_log"]), jnp.repeat(p["d_skip"], SSD_HEAD_DIM).reshape(1, inner),
                p["g_ssd"].reshape(1, inner), e_mat, bsz, seq)

    jj = jnp.arange(SB_T)[:, None]
    ss = jnp.arange(SB_T + LANES)[None, :]
    u_mat = ((ss >= SB_T) | (jj > ss)).astype(BF16)
    y_sb = sb_attn(proj, u_mat, bsz, seq, qcol, kcol, vcol)

    kv = mem_kv(mem.reshape(bsz * mlen, d), p["g_mem"].reshape(1, d), p["w_xkv"].astype(BF16), mlen)

    wr = jnp.pad(p["w_router"], ((0, 0), (0, LANES - n_experts)))
    wrh = wr.astype(BF16)
    wrl = (wr - wrh.astype(F32)).astype(BF16)
    tri = (jnp.arange(merge_tm)[None, :] <= jnp.arange(merge_tm)[:, None]).astype(BF16)
    h2, hn, eidx, rank, gate, counts = merge(
        x2, y_ssd, y_sb, proj, gcol, p["w_ssd_out"].astype(BF16), p["w_sb_out"].astype(BF16),
        p["w_out"].astype(BF16), p["g_xattn"].reshape(1, d), p["w_xq"].astype(BF16), kv,
        p["w_xo"].astype(BF16), p["g_moe"].reshape(1, d), wrh, wrl,
        _pad_lanes(p["b_router"], NEG_BIG), tri, seq, mlen, merge_tm)

    cnt = counts[0, :n_experts]
    padded = (cnt + MOE_BLOCK - 1) // MOE_BLOCK * MOE_BLOCK
    pend = jnp.cumsum(padded)
    pstart = (pend - padded).astype(I32)
    n_blocks = -(-(t * TOP_K) // MOE_BLOCK) + n_experts
    n_rows = n_blocks * MOE_BLOCK
    n_used = (pend[-1] // MOE_BLOCK).astype(I32)
    blk = jnp.minimum(jnp.arange(n_blocks, dtype=I32), n_used - 1) * MOE_BLOCK
    blk_expert = jnp.sum(blk[:, None] >= pend[None, :n_experts - 1], axis=1).astype(I32)
    e_flat = eidx[:, :TOP_K].reshape(-1)
    r_flat = rank[:, :TOP_K].reshape(-1)
    pad_lo = jnp.concatenate([pstart + cnt, pend[-1:]]).astype(I32)
    pad_hi = jnp.concatenate([pend, jnp.full((1,), n_rows)]).astype(I32)

    xin = dispatch(pstart, pad_lo, pad_hi, e_flat, r_flat, hn, n_rows)
    yb = moe_ffn(blk_expert, n_used.reshape(1), xin, p["w_gu"], p["b_gu"][:, None, :],
                 p["w_down"], p["b_down"][:, None, :])
    return h2, (pstart, e_flat, r_flat, yb, gate)


def kernel(x, mem, g_mix, w_in, conv_w, conv_b, dt_bias, a_log, d_skip, g_ssd, w_ssd_out, w_sb_out,
           w_out, g_xattn, g_mem, w_xq, w_xkv, w_xo, g_moe, w_router, b_router, w_gu, b_gu, w_down,
           b_down, g_final):
    stacked = dict(g_mix=g_mix, w_in=w_in, conv_w=conv_w, conv_b=conv_b, dt_bias=dt_bias, a_log=a_log,
                   d_skip=d_skip, g_ssd=g_ssd, w_ssd_out=w_ssd_out, w_sb_out=w_sb_out, w_out=w_out,
                   g_xattn=g_xattn, g_mem=g_mem, w_xq=w_xq, w_xkv=w_xkv, w_xo=w_xo, g_moe=g_moe,
                   w_router=w_router, b_router=b_router, w_gu=w_gu, b_gu=b_gu, w_down=w_down,
                   b_down=b_down)
    assert g_mix.shape[0] == 1, "stacked depth other than 1 is not supported"
    bsz, seq, d = x.shape
    p = {k: v[0] for k, v in stacked.items()}
    h2, (pstart, e_flat, r_flat, yb, gate) = _layer(x, mem, p)
    out = combine(pstart, e_flat, r_flat, yb, h2, gate, g_final.reshape(1, d))
    return out.reshape(bsz, seq, d)
```

```python
import functools

import jax
import jax.numpy as jnp
from jax import lax
from jax.experimental import pallas as pl
from jax.experimental.pallas import tpu as pltpu

F32 = jnp.float32
BF16 = jnp.bfloat16
I32 = jnp.int32

LANES = 128
SUBLANES = 8
EPS = 1e-6
VMEM_LIMIT = 56 * 1024 * 1024

SSD_HEADS = 16
SSD_HEAD_DIM = 64
SSD_GROUPS = 2
SSD_STATE = 128
SSD_CONV = 4
SB_HEADS = 8
SB_HEAD_DIM = 64
XA_HEADS = 4
XA_HEAD_DIM = 128
TOP_K = 4
SWIGLU_LIMIT = 7.0
SWIGLU_ALPHA = 1.702

INPROJ_TN = 1024
SSD_L = 128
SB_T = 256
SB_SKIP = 110.0
MOE_BLOCK = 512
MOE_CAST_ROWS = 128
ZERO_ROWS_LOG2 = 5
ZERO_ROWS = 1 << ZERO_ROWS_LOG2
ROW_DMA_UNROLL = 2
NEG_BIG = -1e30


def _dot(a, b):
    return jnp.dot(a, b, preferred_element_type=F32)


def _dot_nt(a, b):
    return lax.dot_general(a, b, (((1,), (1,)), ((), ())), preferred_element_type=F32)


def _split_bf16(v):
    hi = v.astype(BF16)
    lo = (v - hi.astype(F32)).astype(BF16)
    return hi, lo


def _dot2(v, m):
    hi, lo = _split_bf16(v)
    return _dot(hi, m) + _dot(lo, m)


def _softplus(x):
    return jnp.maximum(x, 0.0) + jnp.log(1.0 + jnp.exp(-jnp.abs(x)))


def _sigmoid(x):
    return 1.0 / (1.0 + jnp.exp(-x))


def _params(sem, vmem=VMEM_LIMIT):
    return pltpu.CompilerParams(dimension_semantics=sem, vmem_limit_bytes=vmem)


def _store_rows(ref, val):
    n = val.shape[0]
    for s in range(SUBLANES):
        ref[pl.ds(s, n, stride=SUBLANES), :] = val[:, s * LANES:(s + 1) * LANES]


def _load_rows(ref, n):
    return jnp.concatenate([ref[pl.ds(s, n, stride=SUBLANES), :] for s in range(SUBLANES)], axis=1)


def _row(ref, r):
    return ref.at[pl.ds(pl.multiple_of(r * SUBLANES, SUBLANES), SUBLANES), :]


def _inproj_kernel(x_ref, g_ref, w_ref, wdt_ref, proj_ref, dt_ref):
    x = x_ref[...]
    n = x * lax.rsqrt(jnp.mean(x * x, axis=-1, keepdims=True) + EPS) * g_ref[...]
    nb = n.astype(BF16)
    dt_ref[...] = _dot(nb, wdt_ref[...])
    for c in range(0, w_ref.shape[1], INPROJ_TN):
        proj_ref[:, c:c + INPROJ_TN] = _dot(nb, w_ref[:, c:c + INPROJ_TN]).astype(BF16)


def inproj(x2, g, w, wdt, tm=512):
    t, d = x2.shape
    n = w.shape[1]
    return pl.pallas_call(
        _inproj_kernel,
        grid=(t // tm,),
        in_specs=[
            pl.BlockSpec((tm, d), lambda i: (i, 0)),
            pl.BlockSpec((1, d), lambda i: (0, 0)),
            pl.BlockSpec((d, n), lambda i: (0, 0), pipeline_mode=pl.Buffered(1)),
            pl.BlockSpec((d, LANES), lambda i: (0, 0)),
        ],
        out_specs=[
            pl.BlockSpec((tm, n), lambda i: (i, 0)),
            pl.BlockSpec((tm, LANES), lambda i: (i, 0)),
        ],
        out_shape=[
            jax.ShapeDtypeStruct((t, n), BF16),
            jax.ShapeDtypeStruct((t, LANES), F32),
        ],
        compiler_params=_params(("arbitrary",)),
        name="inproj",
    )(x2, g, w, wdt)


def _ssd_kernel(z_ref, xs_ref, bc_ref, dt_ref, cw_ref, cb_ref, dtb_ref, alog_ref, dskip_ref,
                gn_ref, e_ref, o_ref, ext_scr, state_scr):
    L = SSD_L
    inner = SSD_HEADS * SSD_HEAD_DIM
    gw = inner // SSD_GROUPS
    n = SSD_STATE

    @pl.when(pl.program_id(1) == 0)
    def _():
        ext_scr[0:SUBLANES, :] = jnp.zeros((SUBLANES, ext_scr.shape[1]), F32)
        state_scr[...] = jnp.zeros(state_scr.shape, F32)

    ext_scr[SUBLANES:SUBLANES + L, 0:inner] = xs_ref[...].astype(F32)
    ext_scr[SUBLANES:SUBLANES + L, inner:] = bc_ref[...].astype(F32)
    conv = cb_ref[...]
    for k in range(SSD_CONV):
        off = SUBLANES - (SSD_CONV - 1) + k
        conv = conv + cw_ref[k:k + 1, :] * ext_scr[off:off + L, :]
    ext_scr[0:SUBLANES, :] = ext_scr[L:L + SUBLANES, :]
    xbc = conv * _sigmoid(conv)
    xs = xbc[:, :inner]

    dt = _softplus(dt_ref[...] + dtb_ref[...])
    da = dt * (-jnp.exp(alog_ref[...]))
    row = lax.broadcasted_iota(I32, (L, L), 0)
    col = lax.broadcasted_iota(I32, (L, L), 1)
    tril = col <= row
    cs = jnp.dot(tril.astype(F32), da, precision=lax.Precision.HIGHEST,
                 preferred_element_type=F32)
    cst = cs.T
    cs_last = cs[L - 1:L, :]

    e = e_ref[...]
    xdt = xs * _dot2(dt, e)
    xw = (xdt * _dot2(jnp.exp(cs_last - cs), e)).astype(BF16)
    od_e = _dot2(jnp.exp(cs), e)
    xdt_b = xdt.astype(BF16)
    lane = lax.broadcasted_iota(I32, (L, LANES), 1)

    y_parts = []
    new_states = []
    for g in range(SSD_GROUPS):
        bg = xbc[:, inner + g * n:inner + (g + 1) * n]
        cg = xbc[:, inner + SSD_GROUPS * n + g * n:inner + SSD_GROUPS * n + (g + 1) * n]
        bg_b = bg.astype(BF16)
        cg_b = cg.astype(BF16)
        cbm = _dot_nt(cg_b, bg_b)
        st = state_scr[:, g * gw:(g + 1) * gw]
        y_off = _dot(cg_b, st.astype(BF16))
        new_states.append(_dot(bg.T.astype(BF16), xw[:, g * gw:(g + 1) * gw]))
        heads_per_group = SSD_HEADS // SSD_GROUPS
        for hp in range(heads_per_group // 2):
            pair = xdt_b[:, g * gw + hp * LANES:g * gw + (hp + 1) * LANES]
            yd = []
            for sub in range(2):
                h = g * heads_per_group + hp * 2 + sub
                seg = cs[:, h:h + 1] - cst[h:h + 1, :]
                lm = jnp.exp(jnp.where(tril, seg, -jnp.inf))
                yd.append(_dot((cbm * lm).astype(BF16), pair))
            y_parts.append((jnp.where(lane < SSD_HEAD_DIM, yd[0], yd[1]), g, hp, y_off))

    cols = []
    for (yd, g, hp, y_off) in y_parts:
        c0 = g * gw + hp * LANES
        cols.append(yd + y_off[:, hp * LANES:(hp + 1) * LANES] * od_e[:, c0:c0 + LANES])
    y = jnp.concatenate(cols, axis=1) + xs * dskip_ref[...]
    zf = z_ref[...].astype(F32)
    y = y * (zf * _sigmoid(zf))
    outs = []
    for g in range(SSD_GROUPS):
        yg = y[:, g * gw:(g + 1) * gw]
        ms = jnp.mean(yg * yg, axis=-1, keepdims=True)
        outs.append(yg * lax.rsqrt(ms + EPS) * gn_ref[:, g * gw:(g + 1) * gw])
    o_ref[...] = jnp.concatenate(outs, axis=1).astype(BF16)

    cd_e = od_e[L - 1:L, :]
    for g in range(SSD_GROUPS):
        state_scr[:, g * gw:(g + 1) * gw] = (
            state_scr[:, g * gw:(g + 1) * gw] * cd_e[:, g * gw:(g + 1) * gw] + new_states[g])


def ssd(proj, dt_raw, conv_w, conv_b, dtb, alog, dskip_e, gn, e_mat, bsz, seq):
    L = SSD_L
    inner = SSD_HEADS * SSD_HEAD_DIM
    bcw = 2 * SSD_GROUPS * SSD_STATE
    nc = seq // L
    ch = inner + bcw
    full = lambda shape: pl.BlockSpec(shape, lambda b, c: (0,) * len(shape))
    return pl.pallas_call(
        _ssd_kernel,
        grid=(bsz, nc),
        in_specs=[
            pl.BlockSpec((L, inner), lambda b, c: (b * nc + c, 0)),
            pl.BlockSpec((L, inner), lambda b, c: (b * nc + c, 1)),
            pl.BlockSpec((L, bcw), lambda b, c: (b * nc + c, 2 * inner // bcw)),
            pl.BlockSpec((L, LANES), lambda b, c: (b * nc + c, 0)),
            full((SSD_CONV, ch)), full((1, ch)), full((1, LANES)), full((1, LANES)),
            full((1, inner)), full((1, inner)), full((LANES, inner)),
        ],
        out_specs=pl.BlockSpec((L, inner), lambda b, c: (b * nc + c, 0)),
        out_shape=jax.ShapeDtypeStruct((bsz * seq, inner), BF16),
        scratch_shapes=[pltpu.VMEM((L + 2 * SUBLANES, ch), F32), pltpu.VMEM((SSD_STATE, inner), F32)],
        compiler_params=_params(("arbitrary", "arbitrary")),
        name="ssd",
    )(proj, proj, proj, dt_raw, conv_w, conv_b, dtb, alog, dskip_e, gn, e_mat)


def _sb_kernel(q_ref, k_ref, v_ref, u_ref, o_ref, carry_scr, acc_scr):
    t = SB_T
    nh = LANES // SB_HEAD_DIM
    i = pl.program_id(2)
    lane = lax.broadcasted_iota(I32, (t, LANES), 1)
    q = q_ref[...]
    scale = SB_HEAD_DIM ** -0.5
    qh = []
    for h in range(nh):
        in_head = (lane >= h * SB_HEAD_DIM) & (lane < (h + 1) * SB_HEAD_DIM)
        qh.append((jnp.where(in_head, q, jnp.zeros_like(q)).astype(F32) * scale).astype(BF16))
    u = u_ref[...]

    def step(j, diagonal, state):
        ks = pl.multiple_of(j * t, t)
        kb = k_ref[pl.ds(ks, t), :]
        vb = v_ref[pl.ds(ks, t), :]
        if diagonal:
            mask = lax.broadcasted_iota(I32, (t, t), 1) < lax.broadcasted_iota(I32, (t, t), 0)
        out = []
        for h in range(nh):
            x = _dot_nt(qh[h], kb)
            sp = _softplus(x)
            if diagonal:
                sp = jnp.where(mask, sp, 0.0)
            suf = _dot(sp.astype(BF16), u)
            e = x - sp - suf[:, :t]
            if state is not None:
                e = e - jnp.concatenate([state[h][0]] * (t // LANES), axis=1)
            w = jnp.exp(e)
            if diagonal:
                w = jnp.where(mask, w, 0.0)
            pv = _dot(w.astype(BF16), vb)
            if state is None:
                out.append((suf[:, t:], pv))
            else:
                out.append((state[h][0] + suf[:, t:], state[h][1] + pv))
        return out

    def smallest(state):
        return functools.reduce(jnp.minimum, [jnp.min(c) for c, _ in state])

    def save(state):
        for h in range(nh):
            carry_scr[h] = state[h][0]
            acc_scr[h] = state[h][1]

    def cond(st):
        j, go = st
        return (j >= 0) & (go > 0)

    def body(st):
        j, _ = st
        state = step(j, False, [(carry_scr[h], acc_scr[h]) for h in range(nh)])
        save(state)
        return j - 1, (smallest(state) < SB_SKIP).astype(I32)

    @pl.when(i == 0)
    def _():
        save(step(0, True, None))

    @pl.when(i > 0)
    def _():
        state = step(i - 1, False, step(i, True, None))
        save(state)
        lax.while_loop(cond, body, (i - 2, (smallest(state) < SB_SKIP).astype(I32)))

    o_ref[...] = jnp.where(lane < SB_HEAD_DIM, acc_scr[0], acc_scr[1]).astype(BF16)


def sb_attn(proj, u_mat, bsz, seq, qcol, kcol, vcol):
    t = SB_T
    nq = seq // t
    npairs = SB_HEADS * SB_HEAD_DIM // LANES
    nh = LANES // SB_HEAD_DIM
    return pl.pallas_call(
        _sb_kernel,
        grid=(bsz, npairs, nq),
        in_specs=[
            pl.BlockSpec((t, LANES), lambda b, p, i: (b * nq + i, qcol + p)),
            pl.BlockSpec((seq, LANES), lambda b, p, i: (b, kcol + p)),
            pl.BlockSpec((seq, LANES), lambda b, p, i: (b, vcol + p)),
            pl.BlockSpec((t, t + LANES), lambda b, p, i: (0, 0)),
        ],
        out_specs=pl.BlockSpec((t, LANES), lambda b, p, i: (b * nq + i, p)),
        out_shape=jax.ShapeDtypeStruct((bsz * seq, npairs * LANES), BF16),
        scratch_shapes=[pltpu.VMEM((nh, t, LANES), F32), pltpu.VMEM((nh, t, LANES), F32)],
        compiler_params=_params(("arbitrary", "arbitrary", "arbitrary")),
        name="sb_attn",
    )(proj, proj, proj, u_mat)


def _memkv_kernel(m_ref, g_ref, w_ref, o_ref):
    x = m_ref[...]
    n = x * lax.rsqrt(jnp.mean(x * x, axis=-1, keepdims=True) + EPS) * g_ref[...]
    o_ref[...] = _dot(n.astype(BF16), w_ref[...]).astype(BF16)


def mem_kv(mem2, g, w, mlen):
    rows, d = mem2.shape
    n = w.shape[1]
    return pl.pallas_call(
        _memkv_kernel,
        grid=(rows // mlen,),
        in_specs=[pl.BlockSpec((mlen, d), lambda b: (b, 0)),
                  pl.BlockSpec((1, d), lambda b: (0, 0)),
                  pl.BlockSpec((d, n), lambda b: (0, 0))],
        out_specs=pl.BlockSpec((mlen, n), lambda b: (b, 0)),
        out_shape=jax.ShapeDtypeStruct((rows, n), BF16),
        compiler_params=_params(("arbitrary",)),
        name="mem_kv",
    )(mem2, g, w)


def _merge_kernel(x_ref, yssd_ref, ysb_ref, gssd_ref, gsb_ref, wso_ref, wsbo_ref, wo_ref, gx_ref,
                  wxq_ref, kv_ref, wxo_ref, gm_ref, wrh_ref, wrl_ref, br_ref, tri_ref,
                  h2_ref, hn_ref, eidx_ref, rank_ref, gate_ref, cnt_ref, cnt_scr):
    @pl.when(pl.program_id(0) == 0)
    def _():
        cnt_scr[...] = jnp.zeros(cnt_scr.shape, F32)

    tm = x_ref.shape[0]
    y_ssd = _dot(yssd_ref[...], wso_ref[...])
    y_sb = _dot(ysb_ref[...], wsbo_ref[...])
    merged = (_sigmoid(gssd_ref[...].astype(F32)) * y_ssd
              + _sigmoid(gsb_ref[...].astype(F32)) * y_sb)
    h1 = x_ref[...] + _dot(merged.astype(BF16), wo_ref[...])

    n2 = h1 * lax.rsqrt(jnp.mean(h1 * h1, axis=-1, keepdims=True) + EPS) * gx_ref[...]
    q = (_dot(n2.astype(BF16), wxq_ref[...]) * (XA_HEAD_DIM ** -0.5)).astype(BF16)
    xi = XA_HEADS * XA_HEAD_DIM
    heads = []
    for h in range(XA_HEADS):
        kh = kv_ref[:, h * XA_HEAD_DIM:(h + 1) * XA_HEAD_DIM]
        vh = kv_ref[:, xi + h * XA_HEAD_DIM:xi + (h + 1) * XA_HEAD_DIM]
        lg = _dot_nt(q[:, h * XA_HEAD_DIM:(h + 1) * XA_HEAD_DIM], kh)
        ex = jnp.exp(lg - jnp.max(lg, axis=-1, keepdims=True))
        den = jnp.sum(ex, axis=-1, keepdims=True)
        heads.append(_dot(ex.astype(BF16), vh) / den)
    o = jnp.concatenate(heads, axis=1).astype(BF16)
    h2 = h1 + _dot(o, wxo_ref[...])
    h2_ref[...] = h2

    n3 = h2 * lax.rsqrt(jnp.mean(h2 * h2, axis=-1, keepdims=True) + EPS) * gm_ref[...]
    _store_rows(hn_ref, n3)

    nh, nl = _split_bf16(n3)
    lg = _dot(nh, wrh_ref[...]) + _dot(nl, wrh_ref[...]) + _dot(nh, wrl_ref[...]) + br_ref[...]
    lane = lax.broadcasted_iota(I32, (tm, LANES), 1)
    lanef = lane.astype(F32)
    work = lg
    vals, idxs = [], []
    for _ in range(TOP_K):
        m = jnp.max(work, axis=-1, keepdims=True)
        idx = jnp.min(jnp.where(work == m, lanef, float(LANES)), axis=-1, keepdims=True)
        vals.append(m)
        idxs.append(idx)
        work = jnp.where(lanef == idx, NEG_BIG * 2.0, work)
    exps = [jnp.exp(v - vals[0]) for v in vals]
    den = exps[0]
    for ex in exps[1:]:
        den = den + ex

    sel = jnp.zeros((tm, LANES), F32)
    for idx in idxs:
        sel = sel + (lanef == idx).astype(F32)
    incl = _dot(tri_ref[...], sel.astype(BF16))
    before = incl - sel + cnt_scr[...]
    eidx = jnp.zeros((tm, LANES), F32)
    rank = jnp.zeros((tm, LANES), F32)
    gate = jnp.zeros((tm, LANES), F32)
    for k in range(TOP_K):
        rk = jnp.sum(jnp.where(lanef == idxs[k], before, 0.0), axis=-1, keepdims=True)
        eidx = jnp.where(lane == k, idxs[k], eidx)
        rank = jnp.where(lane == k, rk, rank)
        gate = jnp.where(lane == k, exps[k] / den, gate)
    eidx_ref[...] = eidx.astype(I32)
    rank_ref[...] = rank.astype(I32)
    gate_ref[...] = gate
    cnt_scr[...] = cnt_scr[...] + incl[tm - 1:tm, :]
    cnt_ref[...] = cnt_scr[...].astype(I32)


def merge(x2, y_ssd, y_sb, proj, gcol, wso, wsbo, wo, gx, wxq, kv, wxo, gm, wrh, wrl, br, tri,
          seq, mlen, tm):
    t, d = x2.shape
    si = y_ssd.shape[1]
    sbi = y_sb.shape[1]
    xi2 = kv.shape[1]
    full = lambda a: pl.BlockSpec(a.shape, lambda i: (0,) * a.ndim)
    row = lambda w: pl.BlockSpec((tm, w), lambda i: (i, 0))
    return pl.pallas_call(
        _merge_kernel,
        grid=(t // tm,),
        in_specs=[
            row(d), row(si), row(sbi),
            pl.BlockSpec((tm, d), lambda i: (i, gcol)),
            pl.BlockSpec((tm, d), lambda i: (i, gcol + 1)),
            full(wso), full(wsbo), full(wo), full(gx), full(wxq),
            pl.BlockSpec((mlen, xi2), lambda i: ((i * tm) // seq, 0)),
            full(wxo), full(gm), full(wrh), full(wrl), full(br), full(tri),
        ],
        out_specs=[
            row(d),
            pl.BlockSpec((tm * d // LANES, LANES), lambda i: (i, 0)),
            row(LANES), row(LANES), row(LANES),
            pl.BlockSpec((1, LANES), lambda i: (0, 0)),
        ],
        out_shape=[
            jax.ShapeDtypeStruct((t, d), F32),
            jax.ShapeDtypeStruct((t * d // LANES, LANES), F32),
            jax.ShapeDtypeStruct((t, LANES), I32),
            jax.ShapeDtypeStruct((t, LANES), I32),
            jax.ShapeDtypeStruct((t, LANES), F32),
            jax.ShapeDtypeStruct((1, LANES), I32),
        ],
        scratch_shapes=[pltpu.VMEM((1, LANES), F32)],
        compiler_params=_params(("arbitrary",)),
        name="merge",
    )(x2, y_ssd, y_sb, proj, proj, wso, wsbo, wo, gx, wxq, kv, wxo, gm, wrh, wrl, br, tri)


def _index_copies(e_hbm, r_hbm, e_sm, r_sm, isem, step, slot):
    n = e_sm.shape[0] // 2
    return (pltpu.make_async_copy(e_hbm.at[pl.ds(step * n, n)], e_sm.at[pl.ds(slot * n, n)], isem.at[0, slot]),
            pltpu.make_async_copy(r_hbm.at[pl.ds(step * n, n)], r_sm.at[pl.ds(slot * n, n)], isem.at[1, slot]))


def _issue_rows(ps_ref, e_sm, r_sm, slot, tm, copy_for):
    n = tm * TOP_K

    def body(it, c):
        toks = [it * ROW_DMA_UNROLL + u for u in range(ROW_DMA_UNROLL)]
        rows = []
        for tok in toks:
            for k in range(TOP_K):
                a = slot * n + tok * TOP_K + k
                rows.append(ps_ref[e_sm[a]] + r_sm[a])
        for u, tok in enumerate(toks):
            for k in range(TOP_K):
                copy_for(tok, k, rows[u * TOP_K + k]).start(priority=k % 2)
        return c

    lax.fori_loop(0, tm // ROW_DMA_UNROLL, body, 0)


def _dispatch_kernel(ps_ref, lo_ref, hi_ref, e_hbm, r_hbm, hn_ref, xin_out, e_sm, r_sm, zero_scr,
                     isem, sem, zsem):
    n = e_sm.shape[0] // 2
    tm = n // TOP_K
    i = pl.program_id(0)
    last = pl.num_programs(0) - 1
    slot = i % 2
    idx = functools.partial(_index_copies, e_hbm, r_hbm, e_sm, r_sm, isem)

    @pl.when(i == 0)
    def _():
        for c in idx(0, 0):
            c.start()

    for c in idx(i, slot):
        c.wait()

    @pl.when(i < last)
    def _():
        for c in idx(i + 1, 1 - slot):
            c.start()

    _issue_rows(ps_ref, e_sm, r_sm, slot, tm,
                lambda tok, k, row: pltpu.make_async_copy(_row(hn_ref, tok), _row(xin_out, row), sem))

    @pl.when(i == last)
    def _():
        zero_scr[...] = jnp.zeros(zero_scr.shape, F32)

        def fill(start, count, rows):
            def copy(j):
                dst = xin_out.at[pl.ds(pl.multiple_of((start + j * rows) * SUBLANES, SUBLANES), rows * SUBLANES), :]
                return pltpu.make_async_copy(zero_scr.at[pl.ds(0, rows * SUBLANES), :], dst, zsem)

            def zissue(j, c2):
                copy(j).start()
                return c2

            def zdrain(j, c2):
                copy(j).wait()
                return c2

            lax.fori_loop(0, count, zissue, 0)
            lax.fori_loop(0, count, zdrain, 0)

        def per_span(s, c):
            lo = lo_ref[s]
            length = hi_ref[s] - lo
            big = lax.shift_right_logical(length, ZERO_ROWS_LOG2)
            fill(lo, big, ZERO_ROWS)
            fill(lo + big * ZERO_ROWS, length - big * ZERO_ROWS, 1)
            return c

        lax.fori_loop(0, lo_ref.shape[0], per_span, 0)

    for k in range(TOP_K):
        pltpu.make_async_copy(hn_ref, xin_out.at[pl.ds(0, tm * SUBLANES), :], sem).wait()


def dispatch(pstart, pad_lo, pad_hi, e_flat, r_flat, hn, n_rows, tm=2048):
    t = hn.shape[0] // SUBLANES
    grid_spec = pltpu.PrefetchScalarGridSpec(
        num_scalar_prefetch=3,
        grid=(t // tm,),
        in_specs=[pl.BlockSpec(memory_space=pl.ANY), pl.BlockSpec(memory_space=pl.ANY),
                  pl.BlockSpec((tm * SUBLANES, LANES), lambda i, ps, lo, hi: (i, 0))],
        out_specs=pl.BlockSpec(memory_space=pl.ANY),
        scratch_shapes=[pltpu.SMEM((2 * tm * TOP_K,), I32), pltpu.SMEM((2 * tm * TOP_K,), I32),
                        pltpu.VMEM((ZERO_ROWS * SUBLANES, LANES), F32),
                        pltpu.SemaphoreType.DMA((2, 2)), pltpu.SemaphoreType.DMA,
                        pltpu.SemaphoreType.DMA],
    )
    return pl.pallas_call(
        _dispatch_kernel,
        grid_spec=grid_spec,
        out_shape=jax.ShapeDtypeStruct((n_rows * SUBLANES, LANES), F32),
        compiler_params=_params(("arbitrary",)),
        name="dispatch",
    )(pstart, pad_lo, pad_hi, e_flat, r_flat, hn)


def _moe_kernel(be_ref, nu_ref, x_ref, wgu_ref, bgu_ref, wd_ref, bd_ref, o_ref, wgu_scr, wd_scr):
    i = pl.program_id(0)
    dff = wd_ref.shape[1]
    active = i < nu_ref[0]

    @pl.when(active & ((i == 0) | (be_ref[i] != be_ref[jnp.maximum(i - 1, 0)])))
    def _():
        for r in range(0, wgu_ref.shape[1], MOE_CAST_ROWS):
            wgu_scr[r:r + MOE_CAST_ROWS, :] = wgu_ref[0, r:r + MOE_CAST_ROWS, :].astype(BF16)
        for r in range(0, dff, MOE_CAST_ROWS):
            wd_scr[r:r + MOE_CAST_ROWS, :] = wd_ref[0, r:r + MOE_CAST_ROWS, :].astype(BF16)

    @pl.when(active)
    def _():
        x = _load_rows(x_ref, MOE_BLOCK).astype(BF16)
        gu = _dot(x, wgu_scr[...]) + bgu_ref[0]
        g_lin = jnp.minimum(gu[:, :dff], SWIGLU_LIMIT)
        u_lin = jnp.clip(gu[:, dff:], -SWIGLU_LIMIT, SWIGLU_LIMIT)
        act = g_lin * _sigmoid(SWIGLU_ALPHA * g_lin) * (u_lin + 1.0)
        _store_rows(o_ref, _dot(act.astype(BF16), wd_scr[...]) + bd_ref[0])

    @pl.when(jnp.logical_not(active))
    def _():
        o_ref[...] = jnp.zeros(o_ref.shape, F32)


def moe_ffn(blk_expert, n_used, xin, wgu, bgu, wd, bd):
    blk_rows = MOE_BLOCK * SUBLANES
    nb = xin.shape[0] // blk_rows
    d = wgu.shape[1]
    f2 = wgu.shape[2]
    dff = wd.shape[1]
    grid_spec = pltpu.PrefetchScalarGridSpec(
        num_scalar_prefetch=2,
        grid=(nb,),
        in_specs=[
            pl.BlockSpec((blk_rows, LANES), lambda i, be, nu: (jnp.minimum(i, nu[0] - 1), 0)),
            pl.BlockSpec((1, d, f2), lambda i, be, nu: (be[i], 0, 0)),
            pl.BlockSpec((1, 1, f2), lambda i, be, nu: (be[i], 0, 0)),
            pl.BlockSpec((1, dff, d), lambda i, be, nu: (be[i], 0, 0)),
            pl.BlockSpec((1, 1, d), lambda i, be, nu: (be[i], 0, 0)),
        ],
        out_specs=pl.BlockSpec((blk_rows, LANES), lambda i, be, nu: (i, 0)),
        scratch_shapes=[pltpu.VMEM((d, f2), BF16), pltpu.VMEM((dff, d), BF16)],
    )
    return pl.pallas_call(
        _moe_kernel,
        grid_spec=grid_spec,
        out_shape=jax.ShapeDtypeStruct(xin.shape, F32),
        compiler_params=_params(("arbitrary",)),
        name="moe_ffn",
    )(blk_expert, n_used, xin, wgu, bgu, wd, bd)


def _combine_kernel(ps_ref, e_hbm, r_hbm, yb_hbm, h2_ref, gate_ref, gf_ref, o_ref,
                    e_sm, r_sm, buf, isem, sem):
    tm = h2_ref.shape[0]
    i = pl.program_id(0)
    last = pl.num_programs(0) - 1
    slot = i % 2
    idx = functools.partial(_index_copies, e_hbm, r_hbm, e_sm, r_sm, isem)

    def gather(s):
        _issue_rows(ps_ref, e_sm, r_sm, s, tm,
                    lambda tok, k, row: pltpu.make_async_copy(_row(yb_hbm, row), _row(buf.at[s, k], tok),
                                                              sem.at[s]))

    @pl.when(i == 0)
    def _():
        for c in idx(0, 0):
            c.start()
        for c in idx(0, 0):
            c.wait()
        gather(0)

        @pl.when(last > 0)
        def _():
            for c in idx(1, 1):
                c.start()

    @pl.when(i < last)
    def _():
        for c in idx(i + 1, 1 - slot):
            c.wait()
        gather(1 - slot)

    @pl.when(i + 2 <= last)
    def _():
        for c in idx(i + 2, slot):
            c.start()

    for k in range(TOP_K):
        pltpu.make_async_copy(yb_hbm.at[pl.ds(0, tm * SUBLANES), :], buf.at[slot, k], sem.at[slot]).wait()

    gate = gate_ref[...]
    h3 = h2_ref[...]
    for k in range(TOP_K):
        h3 = h3 + gate[:, k:k + 1] * _load_rows(buf.at[slot, k], tm)
    o_ref[...] = h3 * lax.rsqrt(jnp.mean(h3 * h3, axis=-1, keepdims=True) + EPS) * gf_ref[...]


def combine(pstart, e_flat, r_flat, yb, h2, gate, gf, tm=512):
    t, d = h2.shape
    grid_spec = pltpu.PrefetchScalarGridSpec(
        num_scalar_prefetch=1,
        grid=(t // tm,),
        in_specs=[
            pl.BlockSpec(memory_space=pl.ANY),
            pl.BlockSpec(memory_space=pl.ANY),
            pl.BlockSpec(memory_space=pl.ANY),
            pl.BlockSpec((tm, d), lambda i, ps: (i, 0)),
            pl.BlockSpec((tm, LANES), lambda i, ps: (i, 0)),
            pl.BlockSpec((1, d), lambda i, ps: (0, 0)),
        ],
        out_specs=pl.BlockSpec((tm, d), lambda i, ps: (i, 0)),
        scratch_shapes=[pltpu.SMEM((2 * tm * TOP_K,), I32), pltpu.SMEM((2 * tm * TOP_K,), I32),
                        pltpu.VMEM((2, TOP_K, tm * SUBLANES, LANES), F32),
                        pltpu.SemaphoreType.DMA((2, 2)), pltpu.SemaphoreType.DMA((2,))],
    )
    return pl.pallas_call(
        _combine_kernel,
        grid_spec=grid_spec,
        out_shape=jax.ShapeDtypeStruct((t, d), F32),
        compiler_params=_params(("arbitrary",)),
        name="combine",
    )(pstart, e_flat, r_flat, yb, h2, gate, gf)


def _pad_lanes(v, fill=0.0):
    return jnp.pad(v.reshape(1, -1).astype(F32), ((0, 0), (0, LANES - v.shape[-1])), constant_values=fill)


def _layer(h, mem, p, merge_tm=512):
    bsz, seq, d = h.shape
    assert d == SUBLANES * LANES, "the row-tile layout needs one (8,128) tile per model row"
    t = bsz * seq
    mlen = mem.shape[1]
    inner = SSD_HEADS * SSD_HEAD_DIM
    conv_ch = inner + 2 * SSD_GROUPS * SSD_STATE
    sbi = SB_HEADS * SB_HEAD_DIM
    n_experts = p["w_router"].shape[1]
    x2 = h.reshape(t, d)

    w_in = p["w_in"]
    dt0 = inner + conv_ch
    w_main = jnp.concatenate([w_in[:, :dt0], w_in[:, dt0 + SSD_HEADS:]], axis=1).astype(BF16)
    w_dt = jnp.pad(w_in[:, dt0:dt0 + SSD_HEADS], ((0, 0), (0, LANES - SSD_HEADS))).astype(BF16)
    proj, dt_raw = inproj(x2, p["g_mix"].reshape(1, d), w_main, w_dt)
    qcol = dt0 // LANES
    kcol = qcol + sbi // LANES
    vcol = kcol + sbi // LANES
    gcol = (dt0 + 3 * sbi) // d

    heads = jnp.arange(LANES)[:, None]
    chan = jnp.arange(inner)[None, :]
    e_mat = (chan // SSD_HEAD_DIM == heads).astype(BF16)
    y_ssd = ssd(proj, dt_raw, p["conv_w"], p["conv_b"].reshape(1, conv_ch), _pad_lanes(p["dt_bias"]),
                _pad_lanes(p["a_log"]), jnp.repeat(p["d_skip"], SSD_HEAD_DIM).reshape(1, inner),
                p["g_ssd"].reshape(1, inner), e_mat, bsz, seq)

    jj = jnp.arange(SB_T)[:, None]
    ss = jnp.arange(SB_T + LANES)[None, :]
    u_mat = ((ss >= SB_T) | (jj > ss)).astype(BF16)
    y_sb = sb_attn(proj, u_mat, bsz, seq, qcol, kcol, vcol)

    kv = mem_kv(mem.reshape(bsz * mlen, d), p["g_mem"].reshape(1, d), p["w_xkv"].astype(BF16), mlen)

    wr = jnp.pad(p["w_router"], ((0, 0), (0, LANES - n_experts)))
    wrh = wr.astype(BF16)
    wrl = (wr - wrh.astype(F32)).astype(BF16)
    tri = (jnp.arange(merge_tm)[None, :] <= jnp.arange(merge_tm)[:, None]).astype(BF16)
    h2, hn, eidx, rank, gate, counts = merge(
        x2, y_ssd, y_sb, proj, gcol, p["w_ssd_out"].astype(BF16), p["w_sb_out"].astype(BF16),
        p["w_out"].astype(BF16), p["g_xattn"].reshape(1, d), p["w_xq"].astype(BF16), kv,
        p["w_xo"].astype(BF16), p["g_moe"].reshape(1, d), wrh, wrl,
        _pad_lanes(p["b_router"], NEG_BIG), tri, seq, mlen, merge_tm)

    cnt = counts[0, :n_experts]
    padded = (cnt + MOE_BLOCK - 1) // MOE_BLOCK * MOE_BLOCK
    pend = jnp.cumsum(padded)
    pstart = (pend - padded).astype(I32)
    n_blocks = -(-(t * TOP_K) // MOE_BLOCK) + n_experts
    n_rows = n_blocks * MOE_BLOCK
    n_used = (pend[-1] // MOE_BLOCK).astype(I32)
    blk = jnp.minimum(jnp.arange(n_blocks, dtype=I32), n_used - 1) * MOE_BLOCK
    blk_expert = jnp.sum(blk[:, None] >= pend[None, :n_experts - 1], axis=1).astype(I32)
    e_flat = eidx[:, :TOP_K].reshape(-1)
    r_flat = rank[:, :TOP_K].reshape(-1)
    pad_lo = jnp.concatenate([pstart + cnt, pend[-1:]]).astype(I32)
    pad_hi = jnp.concatenate([pend, jnp.full((1,), n_rows)]).astype(I32)

    xin = dispatch(pstart, pad_lo, pad_hi, e_flat, r_flat, hn, n_rows)
    yb = moe_ffn(blk_expert, n_used.reshape(1), xin, p["w_gu"], p["b_gu"][:, None, :],
                 p["w_down"], p["b_down"][:, None, :])
    return h2, (pstart, e_flat, r_flat, yb, gate)


def kernel(x, mem, g_mix, w_in, conv_w, conv_b, dt_bias, a_log, d_skip, g_ssd, w_ssd_out, w_sb_out,
           w_out, g_xattn, g_mem, w_xq, w_xkv, w_xo, g_moe, w_router, b_router, w_gu, b_gu, w_down,
           b_down, g_final):
    stacked = dict(g_mix=g_mix, w_in=w_in, conv_w=conv_w, conv_b=conv_b, dt_bias=dt_bias, a_log=a_log,
                   d_skip=d_skip, g_ssd=g_ssd, w_ssd_out=w_ssd_out, w_sb_out=w_sb_out, w_out=w_out,
                   g_xattn=g_xattn, g_mem=g_mem, w_xq=w_xq, w_xkv=w_xkv, w_xo=w_xo, g_moe=g_moe,
                   w_router=w_router, b_router=b_router, w_gu=w_gu, b_gu=b_gu, w_down=w_down,
                   b_down=b_down)
    assert g_mix.shape[0] == 1, "stacked depth other than 1 is not supported"
    bsz, seq, d = x.shape
    p = {k: v[0] for k, v in stacked.items()}
    h2, (pstart, e_flat, r_flat, yb, gate) = _layer(x, mem, p)
    out = combine(pstart, e_flat, r_flat, yb, h2, gate, g_final.reshape(1, d))
    return out.reshape(bsz, seq, d)
```
